```python
import math
import jax
import jax.numpy as jnp
from jax import lax
import numpy as np

D_MODEL = 1024
BATCH = 8
SEQ = 2048
DEPTH = 1
DEC_BATCH = 4
DEC_SEQ = 8192
PAST_LEN = 128

GRID_W = 64
ATT_HEADS = 8
ATT_HEAD_DIM = 64
ATT_WIDTH = ATT_HEADS * ATT_HEAD_DIM
WIN_H_MAX = 8
WIN_W = 16
SSD_HEADS = 8
SSD_HEAD_DIM = 64
SSD_WIDTH = SSD_HEADS * SSD_HEAD_DIM
SSD_GROUPS = 2
SSD_STATE = 128
SSD_CONV = 5
SSD_CHUNK = 128
CONV_CH = SSD_WIDTH + 2 * SSD_GROUPS * SSD_STATE
MIX_WIDTH = ATT_WIDTH + SSD_WIDTH
IN_COLS = 3 * ATT_WIDTH + SSD_WIDTH + CONV_CH + SSD_HEADS
PEER_HEADS = 8
PEER_N_KEYS = 128
PEER_N_EXPERTS = PEER_N_KEYS * PEER_N_KEYS
PEER_D_KEY = 128
PEER_TOPK = 16
PEER_BLOCK = 128
EPS = 1e-6

kernel_name = 'hymba_natten_ssd_peer_encoder'


def rms_norm(x, w):
    xf = x.astype(jnp.float32)
    y = xf * lax.rsqrt(jnp.mean(xf * xf, axis=-1, keepdims=True) + EPS)
    return (y * w.astype(jnp.float32)).astype(x.dtype)


def neighbourhood_attention(q, k, v, rpb):
    b, l, h, dh = q.shape
    rows = l // GRID_W
    kh = min(WIN_H_MAX, rows)
    r = jnp.arange(rows)
    row_start = jnp.clip(r - kh // 2, 0, rows - kh)
    key_rows = row_start[:, None] + jnp.arange(kh)[None, :]
    c = jnp.arange(GRID_W)
    col_start = jnp.clip(c - WIN_W // 2, 0, GRID_W - WIN_W)
    col_in = (c[None, :] >= col_start[:, None]) & (c[None, :] < col_start[:, None] + WIN_W)
    qg = q.reshape(b, rows, GRID_W, h, dh)
    kg = k.reshape(b, rows, GRID_W, h, dh)[:, key_rows]
    vg = v.reshape(b, rows, GRID_W, h, dh)[:, key_rows]
    scores = jnp.einsum('brqhd,brjkhd->bhrqjk', qg, kg,
                        preferred_element_type=jnp.float32) * (dh ** -0.5)
    dr = key_rows - r[:, None] + (WIN_H_MAX - 1)
    dc = jnp.clip(c[None, :] - c[:, None], -(WIN_W - 1), WIN_W - 1) + (WIN_W - 1)
    bias = rpb.astype(jnp.float32)[:, dr[:, None, :, None], dc[None, :, None, :]]
    logits = jnp.where(col_in[:, None, :], scores + bias[None], -jnp.inf)
    probs = jax.nn.softmax(logits.reshape(b, h, rows, GRID_W, kh * GRID_W), axis=-1).reshape(logits.shape)
    out = jnp.einsum('bhrqjk,brjkhd->brqhd', probs.astype(v.dtype), vg)
    return out.reshape(b, l, h * dh)


def centred_depthwise_conv(x, w, bias):
    ch = x.shape[-1]
    y = lax.conv_general_dilated(
        x.astype(jnp.float32), w.astype(jnp.float32)[:, None, :],
        window_strides=(1,), padding=[(SSD_CONV // 2, SSD_CONV // 2)],
        dimension_numbers=('NWC', 'WIO', 'NWC'), feature_group_count=ch)
    return y + bias.astype(jnp.float32)


def segsum(x):
    t = x.shape[-1]
    xr = jnp.broadcast_to(x[..., None], x.shape + (t,))
    xr = jnp.where(jnp.tril(jnp.ones((t, t), dtype=bool), -1), xr, 0.0)
    cs = jnp.cumsum(xr, axis=-2)
    return jnp.where(jnp.tril(jnp.ones((t, t), dtype=bool), 0), cs, -jnp.inf)


def ssd_chunked(x, dt, a, bm, cm):
    b, l, h, p = x.shape
    g, n = bm.shape[-2:]
    r = h // g
    nc = l // SSD_CHUNK
    q = SSD_CHUNK
    dtf = dt.astype(jnp.float32)
    xdt = (x.astype(jnp.float32) * dtf[..., None]).reshape(b, nc, q, g, r, p)
    adt = (dtf * a.astype(jnp.float32)).reshape(b, nc, q, g, r).transpose(0, 3, 4, 1, 2)
    bm = bm.astype(jnp.float32).reshape(b, nc, q, g, n)
    cm = cm.astype(jnp.float32).reshape(b, nc, q, g, n)
    a_cs = jnp.cumsum(adt, axis=-1)
    decay_in = jnp.exp(segsum(adt))
    cb = jnp.einsum('bclgn,bcsgn->bgcls', cm, bm)
    y_diag = jnp.einsum('bgcls,bgrcls,bcsgrp->bclgrp', cb, decay_in, xdt)
    decay_st = jnp.exp(a_cs[..., -1:] - a_cs)
    states = jnp.einsum('bcsgn,bgrcs,bcsgrp->bcgrpn', bm, decay_st, xdt)
    states = jnp.concatenate([jnp.zeros_like(states[:, :1]), states], axis=1)
    chunk_tot = jnp.pad(a_cs[..., -1], ((0, 0), (0, 0), (0, 0), (1, 0)))
    decay_ch = jnp.exp(segsum(chunk_tot))
    states = jnp.einsum('bgrzc,bcgrpn->bzgrpn', decay_ch, states)[:, :-1]
    y_off = jnp.einsum('bclgn,bcgrpn,bgrcl->bclgrp', cm, states, jnp.exp(a_cs))
    return (y_diag + y_off).reshape(b, l, h, p)


def ssd_bidirectional(xbc_raw, z, dt_raw, conv_w, conv_b, dt_bias, a_log, d_skip, ssd_norm_w):
    b, l, _ = xbc_raw.shape
    xbc = jax.nn.silu(centred_depthwise_conv(xbc_raw, conv_w, conv_b))
    xs, bm, cm = jnp.split(xbc, [SSD_WIDTH, SSD_WIDTH + SSD_GROUPS * SSD_STATE], axis=-1)
    xs = xs.reshape(b, l, SSD_HEADS, SSD_HEAD_DIM)
    bm = bm.reshape(b, l, SSD_GROUPS, SSD_STATE)
    cm = cm.reshape(b, l, SSD_GROUPS, SSD_STATE)
    dt_raw = dt_raw.astype(jnp.float32)
    y = xs * d_skip.astype(jnp.float32)[:, None]
    dt_f = jax.nn.softplus(dt_raw + dt_bias[0].astype(jnp.float32))
    y = y + ssd_chunked(xs, dt_f, -jnp.exp(a_log[0].astype(jnp.float32)), bm, cm)
    dt_b = jax.nn.softplus(dt_raw[:, ::-1] + dt_bias[1].astype(jnp.float32))
    y = y + ssd_chunked(xs[:, ::-1], dt_b, -jnp.exp(a_log[1].astype(jnp.float32)),
                        bm[:, ::-1], cm[:, ::-1])[:, ::-1]
    y = y.reshape(b, l, SSD_WIDTH) * jax.nn.silu(z.astype(jnp.float32))
    return rms_norm(y, ssd_norm_w)


def parallel_mixer(h, w_in, rpb, conv_w, conv_b, dt_bias, a_log, d_skip, ssd_norm_w, w_out):
    b, l, _ = h.shape
    proj = h @ w_in
    splits = [ATT_WIDTH, 2 * ATT_WIDTH, 3 * ATT_WIDTH,
              3 * ATT_WIDTH + SSD_WIDTH, 3 * ATT_WIDTH + SSD_WIDTH + CONV_CH]
    q, k, v, z, xbc, dt_raw = jnp.split(proj, splits, axis=-1)
    shp = (b, l, ATT_HEADS, ATT_HEAD_DIM)
    att = neighbourhood_attention(q.reshape(shp), k.reshape(shp), v.reshape(shp), rpb)
    ssd = ssd_bidirectional(xbc, z, dt_raw, conv_w, conv_b, dt_bias, a_log, d_skip, ssd_norm_w)
    mixed = jnp.concatenate([att.astype(jnp.float32), ssd.astype(jnp.float32)], axis=-1)
    return mixed @ w_out


def peer_ffn(h, w_query, sub_keys, expert_u, expert_v):
    b, l, d = h.shape
    blocks = h.reshape(-1, PEER_BLOCK, d)

    def retrieve(xb):
        t = xb.shape[0]
        qry = (xb @ w_query).reshape(t, PEER_HEADS, 2, PEER_D_KEY)
        s = jnp.einsum('thzk,hznk->thzn', qry, sub_keys, preferred_element_type=jnp.float32)
        sv, si = lax.top_k(s, PEER_TOPK)
        cand = sv[:, :, 0, :, None] + sv[:, :, 1, None, :]
        cand_idx = si[:, :, 0, :, None] * PEER_N_KEYS + si[:, :, 1, None, :]
        top_s, top_pos = lax.top_k(cand.reshape(t, PEER_HEADS, PEER_TOPK * PEER_TOPK), PEER_TOPK)
        idx = jnp.take_along_axis(cand_idx.reshape(t, PEER_HEADS, PEER_TOPK * PEER_TOPK), top_pos, axis=-1)
        gate = jax.nn.softmax(top_s, axis=-1)
        u = expert_u[idx]
        v = expert_v[idx]
        act = jax.nn.gelu(jnp.einsum('td,thkd->thk', xb, u, preferred_element_type=jnp.float32),
                          approximate=False) * gate
        return jnp.einsum('thk,thkd->td', act.astype(v.dtype), v)

    return lax.map(retrieve, blocks).reshape(b, l, d)


def trunk(x, norm_mix_w, w_in, rpb, conv_w, conv_b, dt_bias, a_log, d_skip, ssd_norm_w, w_out,
          norm_ffn_w, w_query, sub_keys, expert_u, expert_v, norm_final_w):
    for i in range(DEPTH):
        hm = rms_norm(x, norm_mix_w[i])
        x = x + parallel_mixer(hm, w_in[i], rpb[i], conv_w[i], conv_b[i], dt_bias[i], a_log[i],
                               d_skip[i], ssd_norm_w[i], w_out[i])
        hf = rms_norm(x, norm_ffn_w[i])
        x = x + peer_ffn(hf, w_query[i], sub_keys[i], expert_u[i], expert_v[i])
    return rms_norm(x, norm_final_w)


def setup_inputs(seed: int = 0) -> dict:
    key = jax.random.key(seed)
    ks = jax.random.split(key, 20)
    f32 = jnp.float32

    def nrm(k, shape, scale):
        return jax.random.normal(k, shape, f32) * scale

    u = jax.random.uniform(ks[7], (DEPTH, 2, SSD_HEADS), f32)
    dt0 = jnp.exp(u * (math.log(0.1) - math.log(0.001)) + math.log(0.001))
    dt_bias = dt0 + jnp.log(-jnp.expm1(-dt0))
    a_log = jnp.log(jax.random.uniform(ks[8], (DEPTH, 2, SSD_HEADS), f32, 1.0, 16.0))
    return {
        'x_prompt': nrm(ks[0], (BATCH, SEQ, D_MODEL), 1.0),
        'x_sample': nrm(ks[1], (DEC_BATCH, DEC_SEQ, D_MODEL), 1.0),
        'norm_mix_w': 1.0 + nrm(ks[2], (DEPTH, D_MODEL), 0.01),
        'w_in': nrm(ks[3], (DEPTH, D_MODEL, IN_COLS), D_MODEL ** -0.5),
        'rpb': nrm(ks[4], (DEPTH, ATT_HEADS, 2 * WIN_H_MAX - 1, 2 * WIN_W - 1), 0.02),
        'conv_w': nrm(ks[5], (DEPTH, SSD_CONV, CONV_CH), SSD_CONV ** -0.5),
        'conv_b': nrm(ks[6], (DEPTH, CONV_CH), 0.01),
        'dt_bias': dt_bias,
        'a_log': a_log,
        'd_skip': 1.0 + nrm(ks[9], (DEPTH, SSD_HEADS), 0.1),
        'ssd_norm_w': 1.0 + nrm(ks[10], (DEPTH, SSD_WIDTH), 0.01),
        'w_out': nrm(ks[11], (DEPTH, MIX_WIDTH, D_MODEL), MIX_WIDTH ** -0.5),
        'norm_ffn_w': 1.0 + nrm(ks[12], (DEPTH, D_MODEL), 0.01),
        'w_query': nrm(ks[13], (DEPTH, D_MODEL, PEER_HEADS * 2 * PEER_D_KEY), D_MODEL ** -0.5),
        'sub_keys': nrm(ks[14], (DEPTH, PEER_HEADS, 2, PEER_N_KEYS, PEER_D_KEY), PEER_D_KEY ** -0.5),
        'expert_u': nrm(ks[15], (DEPTH, PEER_N_EXPERTS, D_MODEL), D_MODEL ** -0.5),
        'expert_v': nrm(ks[16], (DEPTH, PEER_N_EXPERTS, D_MODEL), PEER_HEADS ** -0.5),
        'norm_final_w': 1.0 + nrm(ks[17], (D_MODEL,), 0.01),
    }


def reference(x_prompt, x_sample, norm_mix_w, w_in, rpb, conv_w, conv_b, dt_bias, a_log, d_skip,
              ssd_norm_w, w_out, norm_ffn_w, w_query, sub_keys, expert_u, expert_v, norm_final_w):
    y_prompt = trunk(x_prompt, norm_mix_w, w_in, rpb, conv_w, conv_b, dt_bias, a_log, d_skip,
                     ssd_norm_w, w_out, norm_ffn_w, w_query, sub_keys, expert_u, expert_v, norm_final_w)
    y_sample = trunk(x_sample, norm_mix_w, w_in, rpb, conv_w, conv_b, dt_bias, a_log, d_skip,
                     ssd_norm_w, w_out, norm_ffn_w, w_query, sub_keys, expert_u, expert_v, norm_final_w)
    return (y_prompt, y_sample)
```

```python
import functools
import math

import jax
import jax.numpy as jnp
from jax import lax
from jax.experimental import pallas as pl
from jax.experimental.pallas import tpu as pltpu

F32 = jnp.float32
BF16 = jnp.bfloat16

D_MODEL = 1024
GRID_W = 64
ATT_HEADS = 8
ATT_HEAD_DIM = 64
ATT_WIDTH = 512
WIN_H = 8
WIN_W = 16
SSD_HEADS = 8
SSD_HEAD_DIM = 64
SSD_WIDTH = 512
SSD_GROUPS = 2
SSD_STATE = 128
SSD_CONV = 5
SSD_CHUNK = 128
CONV_CH = 1024
MAIN_COLS = 3 * ATT_WIDTH + SSD_WIDTH + CONV_CH
PEER_HEADS = 8
PEER_N_KEYS = 128
PEER_N_EXPERTS = PEER_N_KEYS * PEER_N_KEYS
PEER_D_KEY = 128
PEER_TOPK = 16
EPS = 1e-6

LANES = 128
VMEM_LIMIT = 56 * 1024 * 1024
NEG_BIG = -1e30
NT_DIMS = (((1,), (1,)), ((), ()))


def _cparams(sem):
    return pltpu.CompilerParams(dimension_semantics=sem, vmem_limit_bytes=VMEM_LIMIT)


def _split3(v):
    hi = v.astype(BF16)
    r1 = v - hi.astype(F32)
    mid = r1.astype(BF16)
    lo = (r1 - mid.astype(F32)).astype(BF16)
    return hi, mid, lo


IN_TM = 512


def _inproj_kernel(x_ref, nw_ref, w_ref, wdt_ref, q_ref, k_ref, v_ref, z_ref, xbc_ref, dt_ref):
    x = x_ref[...]
    ms = jnp.mean(x * x, axis=-1, keepdims=True)
    h = (x * lax.rsqrt(ms + EPS) * nw_ref[...]).astype(BF16)

    def proj(lo, hi):
        return jnp.dot(h, w_ref[:, lo:hi], preferred_element_type=F32)

    q_ref[...] = proj(0, 512).astype(BF16)
    k_ref[...] = proj(512, 1024).astype(BF16)
    v_ref[...] = proj(1024, 1536).astype(BF16)
    z_ref[...] = proj(1536, 2048)
    xbc_ref[...] = proj(2048, 3072)
    dt_ref[...] = jnp.dot(h, wdt_ref[...], preferred_element_type=F32)


def _inproj(x, nw, w_main, w_dt):
    n = x.shape[0]
    tm = IN_TM
    row = lambda i: (i, 0)
    full = lambda i: (0, 0)
    return pl.pallas_call(
        _inproj_kernel,
        grid=(n // tm,),
        in_specs=[
            pl.BlockSpec((tm, D_MODEL), row),
            pl.BlockSpec((1, D_MODEL), full),
            pl.BlockSpec((D_MODEL, MAIN_COLS), full),
            pl.BlockSpec((D_MODEL, LANES), full),
        ],
        out_specs=[
            pl.BlockSpec((tm, 512), row),
            pl.BlockSpec((tm, 512), row),
            pl.BlockSpec((tm, 512), row),
            pl.BlockSpec((tm, 512), row),
            pl.BlockSpec((tm, 1024), row),
            pl.BlockSpec((tm, LANES), row),
        ],
        out_shape=[
            jax.ShapeDtypeStruct((n, 512), BF16),
            jax.ShapeDtypeStruct((n, 512), BF16),
            jax.ShapeDtypeStruct((n, 512), BF16),
            jax.ShapeDtypeStruct((n, 512), F32),
            jax.ShapeDtypeStruct((n, 1024), F32),
            jax.ShapeDtypeStruct((n, LANES), F32),
        ],
        compiler_params=_cparams(("parallel",)),
        name="inproj",
    )(x, nw, w_main, w_dt)


ATT_RB = 8
ATT_TOK = ATT_RB * GRID_W
ATT_KEYS = WIN_H * GRID_W


def _attn_kernel(q_ref, kp_ref, kc_ref, kn_ref, vp_ref, vc_ref, vn_ref, bias_ref, o_ref,
                 kbuf, vbuf, *, rows):
    rb = pl.program_id(2)
    kbuf[0:ATT_TOK] = kp_ref[...]
    kbuf[ATT_TOK:2 * ATT_TOK] = kc_ref[...]
    kbuf[2 * ATT_TOK:3 * ATT_TOK] = kn_ref[...]
    vbuf[0:ATT_TOK] = vp_ref[...]
    vbuf[ATT_TOK:2 * ATT_TOK] = vc_ref[...]
    vbuf[2 * ATT_TOK:3 * ATT_TOK] = vn_ref[...]
    lane = lax.broadcasted_iota(jnp.int32, (GRID_W, LANES), 1)
    for r in range(ATT_RB):
        grow = rb * ATT_RB + r
        row_start = jnp.clip(grow - WIN_H // 2, 0, rows - WIN_H)
        off = pl.multiple_of((row_start - (rb - 1) * ATT_RB) * GRID_W, GRID_W)
        var = row_start - grow + (WIN_H - 1)
        kw = kbuf[pl.ds(off, ATT_KEYS), :]
        vw = vbuf[pl.ds(off, ATT_KEYS), :]
        q = q_ref[r * GRID_W:(r + 1) * GRID_W, :]
        acc = jnp.zeros((GRID_W, LANES), F32)
        for hh in range(2):
            inh = (lane >= hh * ATT_HEAD_DIM) & (lane < (hh + 1) * ATT_HEAD_DIM)
            qm = jnp.where(inh, q, jnp.zeros_like(q))
            s = lax.dot_general(qm, kw, NT_DIMS, preferred_element_type=F32)
            s = s * (ATT_HEAD_DIM ** -0.5) + bias_ref[hh, var]
            m = jnp.max(s, axis=-1, keepdims=True)
            p = jnp.exp(s - m)
            l = jnp.sum(p, axis=-1, keepdims=True)
            o = jnp.dot(p.astype(BF16), vw, preferred_element_type=F32) / l
            acc = jnp.where(inh, o, acc)
        o_ref[r * GRID_W:(r + 1) * GRID_W, :] = acc.astype(BF16)


def _attn_bias(rpb):
    c = jnp.arange(GRID_W)
    col_start = jnp.clip(c - WIN_W // 2, 0, GRID_W - WIN_W)
    col_in = (c[None, :] >= col_start[:, None]) & (c[None, :] < col_start[:, None] + WIN_W)
    dc = jnp.clip(c[None, :] - c[:, None], -(WIN_W - 1), WIN_W - 1) + (WIN_W - 1)
    t = rpb.astype(F32)[:, :, dc]
    t = jnp.where(col_in[None, None], t, NEG_BIG)
    idx = jnp.arange(WIN_H)[:, None] + jnp.arange(WIN_H)[None, :]
    t = t[:, idx]
    t = jnp.transpose(t, (0, 1, 3, 2, 4))
    return t.reshape(ATT_HEADS, WIN_H, GRID_W, ATT_KEYS)


def _attention(q, k, v, bias):
    b, l, _ = q.shape
    rows = l // GRID_W
    assert rows >= 2 * WIN_H and rows % ATT_RB == 0
    nrb = rows // ATT_RB
    cur = lambda hp, bi, rb: (bi, rb, hp)
    prv = lambda hp, bi, rb: (bi, jnp.maximum(rb - 1, 0), hp)
    nxt = lambda hp, bi, rb: (bi, jnp.minimum(rb + 1, nrb - 1), hp)
    blk = (None, ATT_TOK, LANES)
    return pl.pallas_call(
        functools.partial(_attn_kernel, rows=rows),
        grid=(ATT_HEADS // 2, b, nrb),
        in_specs=[
            pl.BlockSpec(blk, cur),
            pl.BlockSpec(blk, prv), pl.BlockSpec(blk, cur), pl.BlockSpec(blk, nxt),
            pl.BlockSpec(blk, prv), pl.BlockSpec(blk, cur), pl.BlockSpec(blk, nxt),
            pl.BlockSpec((2, WIN_H, GRID_W, ATT_KEYS), lambda hp, bi, rb: (hp, 0, 0, 0)),
        ],
        out_specs=pl.BlockSpec(blk, cur),
        out_shape=jax.ShapeDtypeStruct((b, l, ATT_WIDTH), BF16),
        scratch_shapes=[pltpu.VMEM((3 * ATT_TOK, LANES), BF16),
                        pltpu.VMEM((3 * ATT_TOK, LANES), BF16)],
        compiler_params=_cparams(("parallel", "parallel", "parallel")),
        name="nbr_attention",
    )(q, k, k, k, v, v, v, bias)


CONV_TL = 512
HALO = 8


def _conv_kernel(xp_ref, xc_ref, xn_ref, w_ref, b_ref, o_ref, buf):
    i = pl.program_id(1)
    nblk = pl.num_programs(1)
    prev = xp_ref[...]
    nxt = xn_ref[...]
    buf[0:HALO] = jnp.where(i == 0, jnp.zeros_like(prev), prev)
    buf[HALO:HALO + CONV_TL] = xc_ref[...]
    buf[HALO + CONV_TL:2 * HALO + CONV_TL] = jnp.where(i == nblk - 1, jnp.zeros_like(nxt), nxt)
    pad = SSD_CONV // 2
    y = jnp.zeros((CONV_TL, CONV_CH), F32) + b_ref[...]
    for j in range(SSD_CONV):
        y = y + buf[HALO - pad + j:HALO - pad + j + CONV_TL, :] * w_ref[j:j + 1, :]
    o_ref[...] = y * (1.0 / (1.0 + jnp.exp(-y)))


def _conv_silu(xbc, conv_w, conv_b):
    b, l, ch = xbc.shape
    tl = CONV_TL
    nblk = l // tl
    per = tl // HALO
    nh = l // HALO
    return pl.pallas_call(
        _conv_kernel,
        grid=(b, nblk),
        in_specs=[
            pl.BlockSpec((None, HALO, ch), lambda bi, i: (bi, jnp.maximum(i * per - 1, 0), 0)),
            pl.BlockSpec((None, tl, ch), lambda bi, i: (bi, i, 0)),
            pl.BlockSpec((None, HALO, ch), lambda bi, i: (bi, jnp.minimum((i + 1) * per, nh - 1), 0)),
            pl.BlockSpec((SSD_CONV, ch), lambda bi, i: (0, 0)),
            pl.BlockSpec((1, ch), lambda bi, i: (0, 0)),
        ],
        out_specs=pl.BlockSpec((None, tl, ch), lambda bi, i: (bi, i, 0)),
        out_shape=jax.ShapeDtypeStruct((b, l, ch), F32),
        scratch_shapes=[pltpu.VMEM((tl + 2 * HALO, ch), F32)],
        compiler_params=_cparams(("parallel", "parallel")),
        name="conv_silu",
    )(xbc, xbc, xbc, conv_w, conv_b)


Q = SSD_CHUNK


def _ssd_direction(xs_ref, bm_ref, cm_ref, dt_ref, bias, a_vec, tri, expand_m, y_ref, s_ref, fwd):
    xx = dt_ref[...] + bias
    dtv = jnp.maximum(xx, 0.0) + jnp.log1p(jnp.exp(-jnp.abs(xx)))
    adt = dtv * a_vec
    hi, mid, lo = _split3(adt)
    cs3 = jnp.dot(tri, jnp.concatenate([hi, mid, lo], axis=1), preferred_element_type=F32)
    cs = cs3[:, 0:LANES] + cs3[:, LANES:2 * LANES] + cs3[:, 2 * LANES:3 * LANES]

    def expand(v):
        h3, m3, l3 = _split3(v)
        r = jnp.dot(jnp.concatenate([h3, m3, l3], axis=0), expand_m, preferred_element_type=F32)
        return r[0:Q] + r[Q:2 * Q] + r[2 * Q:3 * Q]

    dt_x = expand(dtv)
    cs_x = expand(cs)
    last = Q - 1 if fwd else 0
    tot_x = cs_x[last:last + 1, :]
    xdt = xs_ref[...] * dt_x
    xdt_b = xdt.astype(BF16)
    xw = (xdt * jnp.exp(tot_x - cs_x)).astype(BF16)
    scale_off = jnp.exp(cs_x)
    chunk_decay = jnp.exp(tot_x)
    cs_t = cs.T
    row = lax.broadcasted_iota(jnp.int32, (Q, Q), 0)
    col = lax.broadcasted_iota(jnp.int32, (Q, Q), 1)
    tri_mask = (col <= row) if fwd else (col >= row)
    gw = SSD_WIDTH // SSD_GROUPS
    for g in range(SSD_GROUPS):
        bg = bm_ref[:, g * SSD_STATE:(g + 1) * SSD_STATE]
        cg = cm_ref[:, g * SSD_STATE:(g + 1) * SSD_STATE].astype(BF16)
        cb = lax.dot_general(cg, bg.astype(BF16), NT_DIMS, preferred_element_type=F32)
        state = s_ref[g]
        yoff = jnp.dot(cg, state.astype(BF16), preferred_element_type=F32) * scale_off[:, g * gw:(g + 1) * gw]
        bg_t = bg.T.astype(BF16)
        s_ref[g] = state * chunk_decay[:, g * gw:(g + 1) * gw] + jnp.dot(
            bg_t, xw[:, g * gw:(g + 1) * gw], preferred_element_type=F32)
        for pp in range(2):
            pair = 2 * g + pp
            xpair = xdt_b[:, pair * LANES:(pair + 1) * LANES]
            ypair = yoff[:, pp * LANES:(pp + 1) * LANES]
            for hh in range(2):
                hc = 2 * pair + hh + (0 if fwd else SSD_HEADS)
                diff = cs[:, hc:hc + 1] - cs_t[hc:hc + 1, :]
                lm = jnp.where(tri_mask, jnp.exp(jnp.minimum(diff, 0.0)), 0.0)
                mm = (cb * lm).astype(BF16)
                inh = (col >= hh * SSD_HEAD_DIM) & (col < (hh + 1) * SSD_HEAD_DIM)
                ypair = ypair + jnp.dot(mm, jnp.where(inh, xpair, jnp.zeros_like(xpair)),
                                        preferred_element_type=F32)
            y_ref[:, pair * LANES:(pair + 1) * LANES] = ypair


def _ssd_kernel(xs_f, bm_f, cm_f, dt_f, xs_b, bm_b, cm_b, dt_b, bias_ref, alog_ref, tri_ref, e_ref,
                yf_ref, yb_ref, sf_ref, sb_ref):
    @pl.when(pl.program_id(1) == 0)
    def _():
        sf_ref[...] = jnp.zeros_like(sf_ref)
        sb_ref[...] = jnp.zeros_like(sb_ref)

    lane = lax.broadcasted_iota(jnp.int32, (1, LANES), 1)
    bias = bias_ref[...]
    a_vec = jnp.where(lane < 2 * SSD_HEADS, -jnp.exp(alog_ref[...]), 0.0)
    _ssd_direction(xs_f, bm_f, cm_f, dt_f, bias, a_vec, tri_ref[0], e_ref[0], yf_ref, sf_ref, True)
    _ssd_direction(xs_b, bm_b, cm_b, dt_b, bias, a_vec, tri_ref[1], e_ref[1], yb_ref, sb_ref, False)


def _ssd_scan(xbc_act, dt, bias_vec, alog_vec, tri, expand_m):
    b, l, _ = xbc_act.shape
    nc = l // Q
    fw = lambda cb: (lambda bi, c: (bi, c, cb))
    bw = lambda cb: (lambda bi, c: (bi, nc - 1 - c, cb))
    const2 = lambda bi, c: (0, 0)
    const3 = lambda bi, c: (0, 0, 0)

    def chunk_specs(mk):
        return [
            pl.BlockSpec((None, Q, SSD_WIDTH), mk(0)),
            pl.BlockSpec((None, Q, 2 * SSD_STATE), mk(2)),
            pl.BlockSpec((None, Q, 2 * SSD_STATE), mk(3)),
            pl.BlockSpec((None, Q, LANES), mk(0)),
        ]

    return pl.pallas_call(
        _ssd_kernel,
        grid=(b, nc),
        in_specs=chunk_specs(fw) + chunk_specs(bw) + [
            pl.BlockSpec((1, LANES), const2),
            pl.BlockSpec((1, LANES), const2),
            pl.BlockSpec((2, Q, Q), const3),
            pl.BlockSpec((2, LANES, SSD_WIDTH), const3),
        ],
        out_specs=[
            pl.BlockSpec((None, Q, SSD_WIDTH), fw(0)),
            pl.BlockSpec((None, Q, SSD_WIDTH), bw(0)),
        ],
        out_shape=[jax.ShapeDtypeStruct((b, l, SSD_WIDTH), F32)] * 2,
        scratch_shapes=[pltpu.VMEM((SSD_GROUPS, SSD_STATE, 256), F32),
                        pltpu.VMEM((SSD_GROUPS, SSD_STATE, 256), F32)],
        compiler_params=_cparams(("parallel", "arbitrary")),
        name="ssd_scan",
    )(xbc_act, xbc_act, xbc_act, dt, xbc_act, xbc_act, xbc_act, dt, bias_vec, alog_vec, tri, expand_m)


OUT_TM = 512


def _outproj_kernel(x_ref, att_ref, xs_ref, yf_ref, yb_ref, z_ref, dx_ref, snw_ref, wo_ref, nfw_ref,
                    x1_ref, hf_ref):
    z = z_ref[...]
    y = xs_ref[...] * dx_ref[...] + yf_ref[...] + yb_ref[...]
    y = y * (z * (1.0 / (1.0 + jnp.exp(-z))))
    ms = jnp.mean(y * y, axis=-1, keepdims=True)
    ssd = (y * lax.rsqrt(ms + EPS) * snw_ref[...]).astype(BF16)
    mixed = jnp.dot(att_ref[...], wo_ref[0:ATT_WIDTH, :], preferred_element_type=F32)
    mixed = mixed + jnp.dot(ssd, wo_ref[ATT_WIDTH:, :], preferred_element_type=F32)
    x1 = x_ref[...] + mixed
    x1_ref[...] = x1
    ms1 = jnp.mean(x1 * x1, axis=-1, keepdims=True)
    hf_ref[...] = (x1 * lax.rsqrt(ms1 + EPS) * nfw_ref[...]).astype(BF16)


def _outproj(x, att, xbc_act, yf, yb, z, d_x, snw, w_out, nfw):
    n = x.shape[0]
    tm = OUT_TM
    row = lambda i: (i, 0)
    full = lambda i: (0, 0)
    return pl.pallas_call(
        _outproj_kernel,
        grid=(n // tm,),
        in_specs=[
            pl.BlockSpec((tm, D_MODEL), row),
            pl.BlockSpec((tm, ATT_WIDTH), row),
            pl.BlockSpec((tm, SSD_WIDTH), row),
            pl.BlockSpec((tm, SSD_WIDTH), row),
            pl.BlockSpec((tm, SSD_WIDTH), row),
            pl.BlockSpec((tm, SSD_WIDTH), row),
            pl.BlockSpec((1, SSD_WIDTH), full),
            pl.BlockSpec((1, SSD_WIDTH), full),
            pl.BlockSpec((D_MODEL, D_MODEL), full),
            pl.BlockSpec((1, D_MODEL), full),
        ],
        out_specs=[pl.BlockSpec((tm, D_MODEL), row), pl.BlockSpec((tm, D_MODEL), row)],
        out_shape=[jax.ShapeDtypeStruct((n, D_MODEL), F32), jax.ShapeDtypeStruct((n, D_MODEL), BF16)],
        compiler_params=_cparams(("parallel",)),
        name="outproj",
    )(x, att, xbc_act, yf, yb, z, d_x, snw, w_out, nfw)


PEER_TB = 512
PEER_EC = 1024
PEER_M = PEER_EC // PEER_N_KEYS
N_CAND = PEER_TOPK * PEER_TOPK


def _extract_top(w_ref, vals_ref, k):
    def body(it, carry):
        w = w_ref[...]
        m = jnp.max(w, axis=0, keepdims=True)
        vals_ref[pl.ds(it, 1), :] = m
        w_ref[...] = jnp.where(w == m, -jnp.inf, w)
        return carry
    lax.fori_loop(0, k, body, 0)


def _peer_kernel(hf_ref, x1_ref, wqt_ref, sk_ref, u_ref, vt_ref, nw_ref, o_ref,
                 qt_s, s1_s, e1_s, e0_s, t0_s, w_s, cand_s, v0_s, v1_s, c17_s, acc_s):
    c = pl.program_id(1)

    @pl.when(c == 0)
    def _prepare():
        qt_s[...] = lax.dot_general(wqt_ref[...], hf_ref[...], NT_DIMS,
                                    preferred_element_type=F32).astype(BF16)
        acc_s[...] = jnp.zeros_like(acc_s)

        def per_head(h, carry):
            r0 = pl.multiple_of(h * 2 * PEER_D_KEY, 2 * PEER_D_KEY)
            s0 = jnp.dot(sk_ref[2 * h], qt_s[pl.ds(r0, PEER_D_KEY), :], preferred_element_type=F32)
            s1 = jnp.dot(sk_ref[2 * h + 1], qt_s[pl.ds(r0 + PEER_D_KEY, PEER_D_KEY), :],
                         preferred_element_type=F32)
            w_s[...] = s0
            _extract_top(w_s, v0_s, PEER_TOPK)
            w_s[...] = s1
            _extract_top(w_s, v1_s, PEER_TOPK)
            for a in range(PEER_TOPK):
                cand_s[a * PEER_TOPK:(a + 1) * PEER_TOPK, :] = v0_s[a:a + 1, :] + v1_s[...]
            _extract_top(cand_s, c17_s, PEER_TOPK + 1)
            top = c17_s[0:PEER_TOPK, :]
            cmax = c17_s[0:1, :]
            zsum = jnp.sum(jnp.exp(top - cmax), axis=0, keepdims=True)
            tau = 0.5 * (c17_s[PEER_TOPK - 1:PEER_TOPK, :] + c17_s[PEER_TOPK:PEER_TOPK + 1, :])
            s1_s[h] = s1
            e1_s[h] = jnp.exp(s1 - v1_s[0:1, :])
            e0_s[h] = jnp.exp(s0 - v0_s[0:1, :]) / zsum
            t0_s[h] = tau - s0
            return carry

        lax.fori_loop(0, PEER_HEADS, per_head, 0)

    xu = lax.dot_general(u_ref[...], hf_ref[...], NT_DIMS, preferred_element_type=F32)
    act = 0.5 * xu * (1.0 + lax.erf(xu * (2.0 ** -0.5)))
    pieces = []
    for ii in range(PEER_M):
        i = c * PEER_M + ii
        g = jnp.zeros((PEER_N_KEYS, PEER_TB), F32)
        for h in range(PEER_HEADS):
            thr = t0_s[h, pl.ds(i, 1), :]
            coef = e0_s[h, pl.ds(i, 1), :]
            g = g + jnp.where(s1_s[h] >= thr, e1_s[h], 0.0) * coef
        pieces.append((act[ii * PEER_N_KEYS:(ii + 1) * PEER_N_KEYS, :] * g).astype(BF16))
    a_t = jnp.concatenate(pieces, axis=0)
    acc_s[...] += jnp.dot(vt_ref[...], a_t, preferred_element_type=F32)

    @pl.when(c == pl.num_programs(1) - 1)
    def _finish():
        y = x1_ref[...] + acc_s[...].T
        ms = jnp.mean(y * y, axis=-1, keepdims=True)
        o_ref[...] = y * lax.rsqrt(ms + EPS) * nw_ref[...]


def _peer(hf, x1, wq_t, sub_keys, exp_u, exp_vt, nw):
    n = hf.shape[0]
    tb, ec = PEER_TB, PEER_EC
    tok = lambda t, c: (t, 0)
    const2 = lambda t, c: (0, 0)
    return pl.pallas_call(
        _peer_kernel,
        grid=(n // tb, PEER_N_EXPERTS // ec),
        in_specs=[
            pl.BlockSpec((tb, D_MODEL), tok),
            pl.BlockSpec((tb, D_MODEL), tok),
            pl.BlockSpec((2 * PEER_HEADS * PEER_D_KEY, D_MODEL), const2),
            pl.BlockSpec((2 * PEER_HEADS, PEER_N_KEYS, PEER_D_KEY), lambda t, c: (0, 0, 0)),
            pl.BlockSpec((ec, D_MODEL), lambda t, c: (c, 0)),
            pl.BlockSpec((D_MODEL, ec), lambda t, c: (0, c)),
            pl.BlockSpec((1, D_MODEL), const2),
        ],
        out_specs=pl.BlockSpec((tb, D_MODEL), tok),
        out_shape=jax.ShapeDtypeStruct((n, D_MODEL), F32),
        scratch_shapes=[
            pltpu.VMEM((2 * PEER_HEADS * PEER_D_KEY, tb), BF16),
            pltpu.VMEM((PEER_HEADS, PEER_N_KEYS, tb), F32),
            pltpu.VMEM((PEER_HEADS, PEER_N_KEYS, tb), F32),
            pltpu.VMEM((PEER_HEADS, PEER_N_KEYS, tb), F32),
            pltpu.VMEM((PEER_HEADS, PEER_N_KEYS, tb), F32),
            pltpu.VMEM((PEER_N_KEYS, tb), F32),
            pltpu.VMEM((N_CAND, tb), F32),
            pltpu.VMEM((PEER_TOPK, tb), F32),
            pltpu.VMEM((PEER_TOPK, tb), F32),
            pltpu.VMEM((PEER_TOPK + 8, tb), F32),
            pltpu.VMEM((D_MODEL, tb), F32),
        ],
        compiler_params=_cparams(("parallel", "arbitrary")),
        name="peer",
    )(hf, x1, wq_t, sub_keys, exp_u, exp_vt, nw)


def _lane_vec(two_by_heads):
    v = two_by_heads.astype(F32).reshape(1, 2 * SSD_HEADS)
    return jnp.pad(v, ((0, 0), (0, LANES - 2 * SSD_HEADS)))


def _trunk(x, p):
    b, l, d = x.shape
    n = b * l
    xf = x.reshape(n, d)
    q, k, v, z, xbc, dt = _inproj(xf, p["norm_mix_w"], p["w_main"], p["w_dt"])
    att = _attention(q.reshape(b, l, -1), k.reshape(b, l, -1), v.reshape(b, l, -1), p["att_bias"])
    xbc_act = _conv_silu(xbc.reshape(b, l, -1), p["conv_w"], p["conv_b"])
    yf, yb = _ssd_scan(xbc_act, dt.reshape(b, l, -1), p["dt_bias_vec"], p["a_log_vec"], p["tri"], p["expand"])
    x1, hf = _outproj(xf, att.reshape(n, -1), xbc_act.reshape(n, -1), yf.reshape(n, -1), yb.reshape(n, -1),
                      z, p["d_x"], p["ssd_norm_w"], p["w_out"], p["norm_ffn_w"])
    y = _peer(hf, x1, p["wq_t"], p["sub_keys"], p["exp_u"], p["exp_vt"], p["norm_final_w"])
    return y.reshape(b, l, d)


def kernel(x_prompt, x_sample, norm_mix_w, w_in, rpb, conv_w, conv_b, dt_bias, a_log, d_skip, ssd_norm_w,
           w_out, norm_ffn_w, w_query, sub_keys, expert_u, expert_v, norm_final_w):
    assert w_in.shape[0] == 1, "single layer"
    w_in0 = w_in[0]
    w_dt = w_in0[:, MAIN_COLS:]
    w_dt = jnp.concatenate([w_dt, w_dt, jnp.zeros((D_MODEL, LANES - 2 * SSD_HEADS), w_dt.dtype)], axis=1)
    r = jnp.arange(Q)
    tril = (r[None, :] <= r[:, None])
    tri = jnp.stack([tril, tril.T]).astype(BF16)
    hid = jnp.arange(SSD_WIDTH) // SSD_HEAD_DIM
    lane = jnp.arange(LANES)
    expand = jnp.stack([lane[:, None] == hid[None, :],
                        lane[:, None] == hid[None, :] + SSD_HEADS]).astype(BF16)
    p = {
        "norm_mix_w": norm_mix_w[0].reshape(1, -1),
        "w_main": w_in0[:, :MAIN_COLS].astype(BF16),
        "w_dt": w_dt.astype(BF16),
        "att_bias": _attn_bias(rpb[0]),
        "conv_w": conv_w[0],
        "conv_b": conv_b[0].reshape(1, -1),
        "dt_bias_vec": _lane_vec(dt_bias[0]),
        "a_log_vec": _lane_vec(a_log[0]),
        "tri": tri,
        "expand": expand,
        "d_x": jnp.repeat(d_skip[0].astype(F32), SSD_HEAD_DIM).reshape(1, -1),
        "ssd_norm_w": ssd_norm_w[0].reshape(1, -1),
        "w_out": w_out[0].astype(BF16),
        "norm_ffn_w": norm_ffn_w[0].reshape(1, -1),
        "wq_t": w_query[0].T.astype(BF16),
        "sub_keys": sub_keys[0].reshape(2 * PEER_HEADS, PEER_N_KEYS, PEER_D_KEY).astype(BF16),
        "exp_u": expert_u[0].astype(BF16),
        "exp_vt": expert_v[0].T.astype(BF16),
        "norm_final_w": norm_final_w.reshape(1, -1),
    }
    return (_trunk(x_prompt, p), _trunk(x_sample, p))
```

```python
import functools
import math

import jax
import jax.numpy as jnp
from jax import lax
from jax.experimental import pallas as pl
from jax.experimental.pallas import tpu as pltpu

F32 = jnp.float32
BF16 = jnp.bfloat16

D_MODEL = 1024
GRID_W = 64
ATT_HEADS = 8
ATT_HEAD_DIM = 64
ATT_WIDTH = 512
WIN_H = 8
WIN_W = 16
SSD_HEADS = 8
SSD_HEAD_DIM = 64
SSD_WIDTH = 512
SSD_GROUPS = 2
SSD_STATE = 128
SSD_CONV = 5
SSD_CHUNK = 128
CONV_CH = 1024
MAIN_COLS = 3 * ATT_WIDTH + SSD_WIDTH + CONV_CH
PEER_HEADS = 8
PEER_N_KEYS = 128
PEER_N_EXPERTS = PEER_N_KEYS * PEER_N_KEYS
PEER_D_KEY = 128
PEER_TOPK = 16
EPS = 1e-6

LANES = 128
VMEM_LIMIT = 60 * 1024 * 1024
NEG_BIG = -1e30
NT_DIMS = (((1,), (1,)), ((), ()))


def _cparams(sem):
    return pltpu.CompilerParams(dimension_semantics=sem, vmem_limit_bytes=VMEM_LIMIT)


def _split3(v):
    hi = v.astype(BF16)
    r1 = v - hi.astype(F32)
    mid = r1.astype(BF16)
    lo = (r1 - mid.astype(F32)).astype(BF16)
    return hi, mid, lo


IN_TM = 512


def _inproj_kernel(x_ref, nw_ref, w_ref, wdt_ref, q_ref, k_ref, v_ref, z_ref, xbc_ref, dt_ref):
    x = x_ref[...]
    ms = jnp.mean(x * x, axis=-1, keepdims=True)
    h = (x * lax.rsqrt(ms + EPS) * nw_ref[...]).astype(BF16)

    def proj(lo, hi):
        return jnp.dot(h, w_ref[:, lo:hi], preferred_element_type=F32)

    q_ref[...] = proj(0, 512).astype(BF16)
    k_ref[...] = proj(512, 1024).astype(BF16)
    v_ref[...] = proj(1024, 1536).astype(BF16)
    z_ref[...] = proj(1536, 2048)
    xbc_ref[...] = proj(2048, 3072)
    dt_ref[...] = jnp.dot(h, wdt_ref[...], preferred_element_type=F32)


def _inproj(x, nw, w_main, w_dt):
    n = x.shape[0]
    tm = IN_TM
    row = lambda i: (i, 0)
    full = lambda i: (0, 0)
    return pl.pallas_call(
        _inproj_kernel,
        grid=(n // tm,),
        in_specs=[
            pl.BlockSpec((tm, D_MODEL), row),
            pl.BlockSpec((1, D_MODEL), full),
            pl.BlockSpec((D_MODEL, MAIN_COLS), full),
            pl.BlockSpec((D_MODEL, LANES), full),
        ],
        out_specs=[
            pl.BlockSpec((tm, 512), row),
            pl.BlockSpec((tm, 512), row),
            pl.BlockSpec((tm, 512), row),
            pl.BlockSpec((tm, 512), row),
            pl.BlockSpec((tm, 1024), row),
            pl.BlockSpec((tm, LANES), row),
        ],
        out_shape=[
            jax.ShapeDtypeStruct((n, 512), BF16),
            jax.ShapeDtypeStruct((n, 512), BF16),
            jax.ShapeDtypeStruct((n, 512), BF16),
            jax.ShapeDtypeStruct((n, 512), F32),
            jax.ShapeDtypeStruct((n, 1024), F32),
            jax.ShapeDtypeStruct((n, LANES), F32),
        ],
        compiler_params=_cparams(("parallel",)),
        name="inproj",
    )(x, nw, w_main, w_dt)


ATT_RB = 8
ATT_TOK = ATT_RB * GRID_W
ATT_KEYS = WIN_H * GRID_W


def _attn_kernel(q_ref, kp_ref, kc_ref, kn_ref, vp_ref, vc_ref, vn_ref, bias_ref, o_ref,
                 kbuf, vbuf, *, rows):
    rb = pl.program_id(2)
    kbuf[0:ATT_TOK] = kp_ref[...]
    kbuf[ATT_TOK:2 * ATT_TOK] = kc_ref[...]
    kbuf[2 * ATT_TOK:3 * ATT_TOK] = kn_ref[...]
    vbuf[0:ATT_TOK] = vp_ref[...]
    vbuf[ATT_TOK:2 * ATT_TOK] = vc_ref[...]
    vbuf[2 * ATT_TOK:3 * ATT_TOK] = vn_ref[...]
    lane = lax.broadcasted_iota(jnp.int32, (GRID_W, LANES), 1)
    for r in range(ATT_RB):
        grow = rb * ATT_RB + r
        row_start = jnp.clip(grow - WIN_H // 2, 0, rows - WIN_H)
        off = pl.multiple_of((row_start - (rb - 1) * ATT_RB) * GRID_W, GRID_W)
        var = row_start - grow + (WIN_H - 1)
        kw = kbuf[pl.ds(off, ATT_KEYS), :]
        vw = vbuf[pl.ds(off, ATT_KEYS), :]
        q = q_ref[r * GRID_W:(r + 1) * GRID_W, :]
        acc = jnp.zeros((GRID_W, LANES), F32)
        for hh in range(2):
            inh = (lane >= hh * ATT_HEAD_DIM) & (lane < (hh + 1) * ATT_HEAD_DIM)
            qm = jnp.where(inh, q, jnp.zeros_like(q))
            s = lax.dot_general(qm, kw, NT_DIMS, preferred_element_type=F32)
            s = s * (ATT_HEAD_DIM ** -0.5) + bias_ref[hh, var]
            m = jnp.max(s, axis=-1, keepdims=True)
            p = jnp.exp(s - m)
            l = jnp.sum(p, axis=-1, keepdims=True)
            o = jnp.dot(p.astype(BF16), vw, preferred_element_type=F32) / l
            acc = jnp.where(inh, o, acc)
        o_ref[r * GRID_W:(r + 1) * GRID_W, :] = acc.astype(BF16)


def _attn_bias(rpb):
    c = jnp.arange(GRID_W)
    col_start = jnp.clip(c - WIN_W // 2, 0, GRID_W - WIN_W)
    col_in = (c[None, :] >= col_start[:, None]) & (c[None, :] < col_start[:, None] + WIN_W)
    dc = jnp.clip(c[None, :] - c[:, None], -(WIN_W - 1), WIN_W - 1) + (WIN_W - 1)
    t = rpb.astype(F32)[:, :, dc]
    t = jnp.where(col_in[None, None], t, NEG_BIG)
    idx = jnp.arange(WIN_H)[:, None] + jnp.arange(WIN_H)[None, :]
    t = t[:, idx]
    t = jnp.transpose(t, (0, 1, 3, 2, 4))
    return t.reshape(ATT_HEADS, WIN_H, GRID_W, ATT_KEYS)


def _attention(q, k, v, bias):
    b, l, _ = q.shape
    rows = l // GRID_W
    assert rows >= 2 * WIN_H and rows % ATT_RB == 0
    nrb = rows // ATT_RB
    cur = lambda hp, bi, rb: (bi, rb, hp)
    prv = lambda hp, bi, rb: (bi, jnp.maximum(rb - 1, 0), hp)
    nxt = lambda hp, bi, rb: (bi, jnp.minimum(rb + 1, nrb - 1), hp)
    blk = (None, ATT_TOK, LANES)
    return pl.pallas_call(
        functools.partial(_attn_kernel, rows=rows),
        grid=(ATT_HEADS // 2, b, nrb),
        in_specs=[
            pl.BlockSpec(blk, cur),
            pl.BlockSpec(blk, prv), pl.BlockSpec(blk, cur), pl.BlockSpec(blk, nxt),
            pl.BlockSpec(blk, prv), pl.BlockSpec(blk, cur), pl.BlockSpec(blk, nxt),
            pl.BlockSpec((2, WIN_H, GRID_W, ATT_KEYS), lambda hp, bi, rb: (hp, 0, 0, 0)),
        ],
        out_specs=pl.BlockSpec(blk, cur),
        out_shape=jax.ShapeDtypeStruct((b, l, ATT_WIDTH), BF16),
        scratch_shapes=[pltpu.VMEM((3 * ATT_TOK, LANES), BF16),
                        pltpu.VMEM((3 * ATT_TOK, LANES), BF16)],
        compiler_params=_cparams(("parallel", "parallel", "parallel")),
        name="nbr_attention",
    )(q, k, k, k, v, v, v, bias)


CONV_TL = 512
HALO = 8


def _conv_kernel(xp_ref, xc_ref, xn_ref, w_ref, b_ref, o_ref, buf):
    i = pl.program_id(1)
    nblk = pl.num_programs(1)
    prev = xp_ref[...]
    nxt = xn_ref[...]
    buf[0:HALO] = jnp.where(i == 0, jnp.zeros_like(prev), prev)
    buf[HALO:HALO + CONV_TL] = xc_ref[...]
    buf[HALO + CONV_TL:2 * HALO + CONV_TL] = jnp.where(i == nblk - 1, jnp.zeros_like(nxt), nxt)
    pad = SSD_CONV // 2
    y = jnp.zeros((CONV_TL, CONV_CH), F32) + b_ref[...]
    for j in range(SSD_CONV):
        y = y + buf[HALO - pad + j:HALO - pad + j + CONV_TL, :] * w_ref[j:j + 1, :]
    o_ref[...] = y * (1.0 / (1.0 + jnp.exp(-y)))


def _conv_silu(xbc, conv_w, conv_b):
    b, l, ch = xbc.shape
    tl = CONV_TL
    nblk = l // tl
    per = tl // HALO
    nh = l // HALO
    return pl.pallas_call(
        _conv_kernel,
        grid=(b, nblk),
        in_specs=[
            pl.BlockSpec((None, HALO, ch), lambda bi, i: (bi, jnp.maximum(i * per - 1, 0), 0)),
            pl.BlockSpec((None, tl, ch), lambda bi, i: (bi, i, 0)),
            pl.BlockSpec((None, HALO, ch), lambda bi, i: (bi, jnp.minimum((i + 1) * per, nh - 1), 0)),
            pl.BlockSpec((SSD_CONV, ch), lambda bi, i: (0, 0)),
            pl.BlockSpec((1, ch), lambda bi, i: (0, 0)),
        ],
        out_specs=pl.BlockSpec((None, tl, ch), lambda bi, i: (bi, i, 0)),
        out_shape=jax.ShapeDtypeStruct((b, l, ch), F32),
        scratch_shapes=[pltpu.VMEM((tl + 2 * HALO, ch), F32)],
        compiler_params=_cparams(("parallel", "parallel")),
        name="conv_silu",
    )(xbc, xbc, xbc, conv_w, conv_b)


Q = SSD_CHUNK


def _ssd_direction(xs_ref, bm_ref, cm_ref, dt_ref, bias, a_vec, tri, expand_m, y_ref, s_ref, fwd):
    xx = dt_ref[...] + bias
    dtv = jnp.maximum(xx, 0.0) + jnp.log1p(jnp.exp(-jnp.abs(xx)))
    adt = dtv * a_vec
    hi, mid, lo = _split3(adt)
    cs3 = jnp.dot(tri, jnp.concatenate([hi, mid, lo], axis=1), preferred_element_type=F32)
    cs = cs3[:, 0:LANES] + cs3[:, LANES:2 * LANES] + cs3[:, 2 * LANES:3 * LANES]

    def expand(v):
        h3, m3, l3 = _split3(v)
        r = jnp.dot(jnp.concatenate([h3, m3, l3], axis=0), expand_m, preferred_element_type=F32)
        return r[0:Q] + r[Q:2 * Q] + r[2 * Q:3 * Q]

    dt_x = expand(dtv)
    cs_x = expand(cs)
    last = Q - 1 if fwd else 0
    tot_x = cs_x[last:last + 1, :]
    xdt = xs_ref[...] * dt_x
    xdt_b = xdt.astype(BF16)
    xw = (xdt * jnp.exp(tot_x - cs_x)).astype(BF16)
    scale_off = jnp.exp(cs_x)
    chunk_decay = jnp.exp(tot_x)
    cs_t = cs.T
    row = lax.broadcasted_iota(jnp.int32, (Q, Q), 0)
    col = lax.broadcasted_iota(jnp.int32, (Q, Q), 1)
    tri_mask = (col <= row) if fwd else (col >= row)
    gw = SSD_WIDTH // SSD_GROUPS
    for g in range(SSD_GROUPS):
        bg = bm_ref[:, g * SSD_STATE:(g + 1) * SSD_STATE]
        cg = cm_ref[:, g * SSD_STATE:(g + 1) * SSD_STATE].astype(BF16)
        cb = lax.dot_general(cg, bg.astype(BF16), NT_DIMS, preferred_element_type=F32)
        state = s_ref[g]
        yoff = jnp.dot(cg, state.astype(BF16), preferred_element_type=F32) * scale_off[:, g * gw:(g + 1) * gw]
        bg_t = bg.T.astype(BF16)
        s_ref[g] = state * chunk_decay[:, g * gw:(g + 1) * gw] + jnp.dot(
            bg_t, xw[:, g * gw:(g + 1) * gw], preferred_element_type=F32)
        for pp in range(2):
            pair = 2 * g + pp
            xpair = xdt_b[:, pair * LANES:(pair + 1) * LANES]
            ypair = yoff[:, pp * LANES:(pp + 1) * LANES]
            for hh in range(2):
                hc = 2 * pair + hh + (0 if fwd else SSD_HEADS)
                diff = cs[:, hc:hc + 1] - cs_t[hc:hc + 1, :]
                lm = jnp.where(tri_mask, jnp.exp(jnp.minimum(diff, 0.0)), 0.0)
                mm = (cb * lm).astype(BF16)
                inh = (col >= hh * SSD_HEAD_DIM) & (col < (hh + 1) * SSD_HEAD_DIM)
                ypair = ypair + jnp.dot(mm, jnp.where(inh, xpair, jnp.zeros_like(xpair)),
                                        preferred_element_type=F32)
            y_ref[:, pair * LANES:(pair + 1) * LANES] = ypair


def _ssd_kernel(xs_f, bm_f, cm_f, dt_f, xs_b, bm_b, cm_b, dt_b, bias_ref, alog_ref, tri_ref, e_ref,
                yf_ref, yb_ref, sf_ref, sb_ref):
    @pl.when(pl.program_id(1) == 0)
    def _():
        sf_ref[...] = jnp.zeros_like(sf_ref)
        sb_ref[...] = jnp.zeros_like(sb_ref)

    lane = lax.broadcasted_iota(jnp.int32, (1, LANES), 1)
    bias = bias_ref[...]
    a_vec = jnp.where(lane < 2 * SSD_HEADS, -jnp.exp(alog_ref[...]), 0.0)
    _ssd_direction(xs_f, bm_f, cm_f, dt_f, bias, a_vec, tri_ref[0], e_ref[0], yf_ref, sf_ref, True)
    _ssd_direction(xs_b, bm_b, cm_b, dt_b, bias, a_vec, tri_ref[1], e_ref[1], yb_ref, sb_ref, False)


def _ssd_scan(xbc_act, dt, bias_vec, alog_vec, tri, expand_m):
    b, l, _ = xbc_act.shape
    nc = l // Q
    fw = lambda cb: (lambda bi, c: (bi, c, cb))
    bw = lambda cb: (lambda bi, c: (bi, nc - 1 - c, cb))
    const2 = lambda bi, c: (0, 0)
    const3 = lambda bi, c: (0, 0, 0)

    def chunk_specs(mk):
        return [
            pl.BlockSpec((None, Q, SSD_WIDTH), mk(0)),
            pl.BlockSpec((None, Q, 2 * SSD_STATE), mk(2)),
            pl.BlockSpec((None, Q, 2 * SSD_STATE), mk(3)),
            pl.BlockSpec((None, Q, LANES), mk(0)),
        ]

    return pl.pallas_call(
        _ssd_kernel,
        grid=(b, nc),
        in_specs=chunk_specs(fw) + chunk_specs(bw) + [
            pl.BlockSpec((1, LANES), const2),
            pl.BlockSpec((1, LANES), const2),
            pl.BlockSpec((2, Q, Q), const3),
            pl.BlockSpec((2, LANES, SSD_WIDTH), const3),
        ],
        out_specs=[
            pl.BlockSpec((None, Q, SSD_WIDTH), fw(0)),
            pl.BlockSpec((None, Q, SSD_WIDTH), bw(0)),
        ],
        out_shape=[jax.ShapeDtypeStruct((b, l, SSD_WIDTH), F32)] * 2,
        scratch_shapes=[pltpu.VMEM((SSD_GROUPS, SSD_STATE, 256), F32),
                        pltpu.VMEM((SSD_GROUPS, SSD_STATE, 256), F32)],
        compiler_params=_cparams(("parallel", "arbitrary")),
        name="ssd_scan",
    )(xbc_act, xbc_act, xbc_act, dt, xbc_act, xbc_act, xbc_act, dt, bias_vec, alog_vec, tri, expand_m)


OUT_TM = 512


def _outproj_kernel(x_ref, att_ref, xs_ref, yf_ref, yb_ref, z_ref, dx_ref, snw_ref, wo_ref, x1_ref):
    z = z_ref[...]
    y = xs_ref[...] * dx_ref[...] + yf_ref[...] + yb_ref[...]
    y = y * (z * (1.0 / (1.0 + jnp.exp(-z))))
    ms = jnp.mean(y * y, axis=-1, keepdims=True)
    ssd = (y * lax.rsqrt(ms + EPS) * snw_ref[...]).astype(BF16)
    mixed = jnp.dot(att_ref[...], wo_ref[0:ATT_WIDTH, :], preferred_element_type=F32)
    mixed = mixed + jnp.dot(ssd, wo_ref[ATT_WIDTH:, :], preferred_element_type=F32)
    x1_ref[...] = x_ref[...] + mixed


def _outproj(x, att, xbc_act, yf, yb, z, d_x, snw, w_out):
    n = x.shape[0]
    tm = OUT_TM
    row = lambda i: (i, 0)
    full = lambda i: (0, 0)
    return pl.pallas_call(
        _outproj_kernel,
        grid=(n // tm,),
        in_specs=[
            pl.BlockSpec((tm, D_MODEL), row),
            pl.BlockSpec((tm, ATT_WIDTH), row),
            pl.BlockSpec((tm, SSD_WIDTH), row),
            pl.BlockSpec((tm, SSD_WIDTH), row),
            pl.BlockSpec((tm, SSD_WIDTH), row),
            pl.BlockSpec((tm, SSD_WIDTH), row),
            pl.BlockSpec((1, SSD_WIDTH), full),
            pl.BlockSpec((1, SSD_WIDTH), full),
            pl.BlockSpec((D_MODEL, D_MODEL), full),
        ],
        out_specs=pl.BlockSpec((tm, D_MODEL), row),
        out_shape=jax.ShapeDtypeStruct((n, D_MODEL), F32),
        compiler_params=_cparams(("parallel",)),
        name="outproj",
    )(x, att, xbc_act, yf, yb, z, d_x, snw, w_out)


PEER_TB = 1024
PEER_PIECE = 256
PEER_EC = 1024
PEER_M = PEER_EC // PEER_N_KEYS
BF16_ROWS = 16
N_JT = PEER_N_KEYS // BF16_ROWS
CAND_ROWS = PEER_TOPK + 8 * 7 + 8


def _extract_top(w_ref, vals_ref, k, rank_ref=None):
    def body(it, carry):
        w = w_ref[...]
        m = jnp.max(w, axis=0, keepdims=True)
        vals_ref[pl.ds(it, 1), :] = m
        hit = w == m
        if rank_ref is not None:
            rank_ref[...] = jnp.where(hit, it.astype(F32), rank_ref[...])
        w_ref[...] = jnp.where(hit, -jnp.inf, w)
        return carry
    lax.fori_loop(0, k, body, 0)


def _pair_candidates(v0_s, v1_s):
    rows = [v0_s[0:1, :] + v1_s[...]]
    for a in range(1, 8):
        rows.append(v0_s[a:a + 1, :] + v1_s[0:8, :])
    rows.append(v0_s[8:16, :] + v1_s[0:1, :])
    return jnp.concatenate(rows, axis=0)


def _peer_kernel(x1_ref, fw_ref, wqt_ref, sk_ref, u_ref, vt_ref, nw_ref, o_ref,
                 hf_s, qt_s, r1_s, e1_s, n_s, c_s, w_s, rk_s, cand_s, v0_s, v1_s, c16_s, acc_s):
    c = pl.program_id(1)

    @pl.when(c == 0)
    def _prepare():
        x1 = x1_ref[...]
        ms = jnp.mean(x1 * x1, axis=-1, keepdims=True)
        hf_s[...] = (x1 * lax.rsqrt(ms + EPS) * fw_ref[...]).astype(BF16)
        qt_s[...] = lax.dot_general(wqt_ref[...], hf_s[...], NT_DIMS,
                                    preferred_element_type=F32).astype(BF16)
        acc_s[...] = jnp.zeros_like(acc_s)

        def per_head(h, carry):
            r0 = pl.multiple_of(h * 2 * PEER_D_KEY, 2 * PEER_D_KEY)
            s0 = jnp.dot(sk_ref[2 * h], qt_s[pl.ds(r0, PEER_D_KEY), :], preferred_element_type=F32)
            s1 = jnp.dot(sk_ref[2 * h + 1], qt_s[pl.ds(r0 + PEER_D_KEY, PEER_D_KEY), :],
                         preferred_element_type=F32)
            w_s[...] = s0
            _extract_top(w_s, v0_s, PEER_TOPK)
            w_s[...] = s1
            rk_s[...] = jnp.full(rk_s.shape, float(PEER_TOPK), F32)
            _extract_top(w_s, v1_s, PEER_TOPK, rk_s)
            cand_s[...] = _pair_candidates(v0_s, v1_s)
            _extract_top(cand_s, c16_s, PEER_TOPK)
            tau = c16_s[PEER_TOPK - 1:PEER_TOPK, :]
            cmax = c16_s[0:1, :]
            cand = _pair_candidates(v0_s, v1_s)
            zsum = jnp.sum(jnp.where(cand >= tau, jnp.exp(cand - cmax), 0.0), axis=0, keepdims=True)
            n = jnp.zeros_like(s0)
            for b in range(PEER_TOPK):
                n = jnp.where(s0 + v1_s[b:b + 1, :] >= tau, float(b + 1), n)
            n = jnp.where(s0 >= v0_s[PEER_TOPK - 1:PEER_TOPK, :], n, 0.0)
            n_s[h] = n
            c_s[h] = jnp.exp(s0 - v0_s[0:1, :]) / zsum
            r1_s[h] = rk_s[...].astype(BF16).reshape(N_JT, BF16_ROWS, PEER_TB)
            e1_s[h] = jnp.exp(s1 - v1_s[0:1, :]).astype(BF16).reshape(N_JT, BF16_ROWS, PEER_TB)
            return carry

        lax.fori_loop(0, PEER_HEADS, per_head, 0)

    n_pieces = PEER_TB // PEER_PIECE

    def expert_scores(p):
        return lax.dot_general(u_ref[...], hf_s[p * PEER_PIECE:(p + 1) * PEER_PIECE, :], NT_DIMS,
                               preferred_element_type=F32)

    xu_next = expert_scores(0)
    for p in range(n_pieces):
        lanes = slice(p * PEER_PIECE, (p + 1) * PEER_PIECE)
        xu = xu_next
        if p + 1 < n_pieces:
            xu_next = expert_scores(p + 1)
        act = (0.5 * xu * (1.0 + lax.erf(xu * (2.0 ** -0.5)))).astype(BF16)
        pieces = []
        for ii in range(PEER_M):
            i = c * PEER_M + ii
            g = jnp.zeros((N_JT, BF16_ROWS, PEER_PIECE), BF16)
            for h in range(PEER_HEADS):
                nb = jnp.broadcast_to(n_s[h, pl.ds(i, 1), lanes], (BF16_ROWS, PEER_PIECE)).astype(BF16)
                cb = jnp.broadcast_to(c_s[h, pl.ds(i, 1), lanes], (BF16_ROWS, PEER_PIECE)).astype(BF16)
                sel = jnp.where(r1_s[h, :, :, lanes] < nb[None], e1_s[h, :, :, lanes], jnp.zeros((), BF16))
                g = g + sel * cb[None]
            pieces.append(act[ii * PEER_N_KEYS:(ii + 1) * PEER_N_KEYS, :] * g.reshape(PEER_N_KEYS, PEER_PIECE))
        a_t = jnp.concatenate(pieces, axis=0)
        acc_s[:, lanes] += jnp.dot(vt_ref[...], a_t, preferred_element_type=F32)

    @pl.when(c == pl.num_programs(1) - 1)
    def _finish():
        y = x1_ref[...] + acc_s[...].T
        ms = jnp.mean(y * y, axis=-1, keepdims=True)
        o_ref[...] = y * lax.rsqrt(ms + EPS) * nw_ref[...]


def _peer(x1, fw, wq_t, sub_keys, exp_u, exp_vt, nw):
    n = x1.shape[0]
    tb, ec = PEER_TB, PEER_EC
    tok = lambda t, c: (t, 0)
    const2 = lambda t, c: (0, 0)
    head_tiles = (PEER_HEADS, N_JT, BF16_ROWS, tb)
    return pl.pallas_call(
        _peer_kernel,
        grid=(n // tb, PEER_N_EXPERTS // ec),
        in_specs=[
            pl.BlockSpec((tb, D_MODEL), tok, pipeline_mode=pl.Buffered(1)),
            pl.BlockSpec((1, D_MODEL), const2),
            pl.BlockSpec((2 * PEER_HEADS * PEER_D_KEY, D_MODEL), const2, pipeline_mode=pl.Buffered(1)),
            pl.BlockSpec((2 * PEER_HEADS, PEER_N_KEYS, PEER_D_KEY), lambda t, c: (0, 0, 0)),
            pl.BlockSpec((ec, D_MODEL), lambda t, c: (c, 0)),
            pl.BlockSpec((D_MODEL, ec), lambda t, c: (0, c)),
            pl.BlockSpec((1, D_MODEL), const2),
        ],
        out_specs=pl.BlockSpec((tb, D_MODEL), tok),
        out_shape=jax.ShapeDtypeStruct((n, D_MODEL), F32),
        scratch_shapes=[
            pltpu.VMEM((tb, D_MODEL), BF16),
            pltpu.VMEM((2 * PEER_HEADS * PEER_D_KEY, tb), BF16),
            pltpu.VMEM(head_tiles, BF16),
            pltpu.VMEM(head_tiles, BF16),
            pltpu.VMEM((PEER_HEADS, PEER_N_KEYS, tb), F32),
            pltpu.VMEM((PEER_HEADS, PEER_N_KEYS, tb), F32),
            pltpu.VMEM((PEER_N_KEYS, tb), F32),
            pltpu.VMEM((PEER_N_KEYS, tb), F32),
            pltpu.VMEM((CAND_ROWS, tb), F32),
            pltpu.VMEM((PEER_TOPK, tb), F32),
            pltpu.VMEM((PEER_TOPK, tb), F32),
            pltpu.VMEM((PEER_TOPK, tb), F32),
            pltpu.VMEM((D_MODEL, tb), F32),
        ],
        compiler_params=_cparams(("parallel", "arbitrary")),
        name="peer",
    )(x1, fw, wq_t, sub_keys, exp_u, exp_vt, nw)


def _lane_vec(two_by_heads):
    v = two_by_heads.astype(F32).reshape(1, 2 * SSD_HEADS)
    return jnp.pad(v, ((0, 0), (0, LANES - 2 * SSD_HEADS)))


def _trunk(x, p):
    b, l, d = x.shape
    n = b * l
    xf = x.reshape(n, d)
    q, k, v, z, xbc, dt = _inproj(xf, p["norm_mix_w"], p["w_main"], p["w_dt"])
    att = _attention(q.reshape(b, l, -1), k.reshape(b, l, -1), v.reshape(b, l, -1), p["att_bias"])
    xbc_act = _conv_silu(xbc.reshape(b, l, -1), p["conv_w"], p["conv_b"])
    yf, yb = _ssd_scan(xbc_act, dt.reshape(b, l, -1), p["dt_bias_vec"], p["a_log_vec"], p["tri"], p["expand"])
    x1 = _outproj(xf, att.reshape(n, -1), xbc_act.reshape(n, -1), yf.reshape(n, -1), yb.reshape(n, -1),
                  z, p["d_x"], p["ssd_norm_w"], p["w_out"])
    y = _peer(x1, p["norm_ffn_w"], p["wq_t"], p["sub_keys"], p["exp_u"], p["exp_vt"], p["norm_final_w"])
    return y.reshape(b, l, d)


def kernel(x_prompt, x_sample, norm_mix_w, w_in, rpb, conv_w, conv_b, dt_bias, a_log, d_skip, ssd_norm_w,
           w_out, norm_ffn_w, w_query, sub_keys, expert_u, expert_v, norm_final_w):
    assert w_in.shape[0] == 1, "single layer"
    w_in0 = w_in[0]
    w_dt = w_in0[:, MAIN_COLS:]
    w_dt = jnp.concatenate([w_dt, w_dt, jnp.zeros((D_MODEL, LANES - 2 * SSD_HEADS), w_dt.dtype)], axis=1)
    r = jnp.arange(Q)
    tril = (r[None, :] <= r[:, None])
    tri = jnp.stack([tril, tril.T]).astype(BF16)
    hid = jnp.arange(SSD_WIDTH) // SSD_HEAD_DIM
    lane = jnp.arange(LANES)
    expand = jnp.stack([lane[:, None] == hid[None, :],
                        lane[:, None] == hid[None, :] + SSD_HEADS]).astype(BF16)
    p = {
        "norm_mix_w": norm_mix_w[0].reshape(1, -1),
        "w_main": w_in0[:, :MAIN_COLS].astype(BF16),
        "w_dt": w_dt.astype(BF16),
        "att_bias": _attn_bias(rpb[0]),
        "conv_w": conv_w[0],
        "conv_b": conv_b[0].reshape(1, -1),
        "dt_bias_vec": _lane_vec(dt_bias[0]),
        "a_log_vec": _lane_vec(a_log[0]),
        "tri": tri,
        "expand": expand,
        "d_x": jnp.repeat(d_skip[0].astype(F32), SSD_HEAD_DIM).reshape(1, -1),
        "ssd_norm_w": ssd_norm_w[0].reshape(1, -1),
        "w_out": w_out[0].astype(BF16),
        "norm_ffn_w": norm_ffn_w[0].reshape(1, -1),
        "wq_t": w_query[0].T.astype(BF16),
        "sub_keys": sub_keys[0].reshape(2 * PEER_HEADS, PEER_N_KEYS, PEER_D_KEY).astype(BF16),
        "exp_u": expert_u[0].astype(BF16),
        "exp_vt": expert_v[0].T.astype(BF16),
        "norm_final_w": norm_final_w.reshape(1, -1),
    }
    return (_trunk(x_prompt, p), _trunk(x_sample, p))
```

```python
import functools
import math

import jax
import jax.numpy as jnp
from jax import lax
from jax.experimental import pallas as pl
from jax.experimental.pallas import tpu as pltpu

F32 = jnp.float32
BF16 = jnp.bfloat16

D_MODEL = 1024
GRID_W = 64
ATT_HEADS = 8
ATT_HEAD_DIM = 64
ATT_WIDTH = 512
WIN_H = 8
WIN_W = 16
SSD_HEADS = 8
SSD_HEAD_DIM = 64
SSD_WIDTH = 512
SSD_GROUPS = 2
SSD_STATE = 128
SSD_CONV = 5
SSD_CHUNK = 128
CONV_CH = 1024
MAIN_COLS = 3 * ATT_WIDTH + SSD_WIDTH + CONV_CH
PEER_HEADS = 8
PEER_N_KEYS = 128
PEER_N_EXPERTS = PEER_N_KEYS * PEER_N_KEYS
PEER_D_KEY = 128
PEER_TOPK = 16
EPS = 1e-6

LANES = 128
VMEM_LIMIT = 60 * 1024 * 1024
NEG_BIG = -1e30
NT_DIMS = (((1,), (1,)), ((), ()))


def _cparams(sem):
    return pltpu.CompilerParams(dimension_semantics=sem, vmem_limit_bytes=VMEM_LIMIT)


def _split3(v):
    hi = v.astype(BF16)
    r1 = v - hi.astype(F32)
    mid = r1.astype(BF16)
    lo = (r1 - mid.astype(F32)).astype(BF16)
    return hi, mid, lo


IN_TM = 512


def _inproj_kernel(x_ref, nw_ref, w_ref, wdt_ref, q_ref, k_ref, v_ref, z_ref, xbc_ref, dt_ref):
    x = x_ref[...]
    ms = jnp.mean(x * x, axis=-1, keepdims=True)
    h = (x * lax.rsqrt(ms + EPS) * nw_ref[...]).astype(BF16)

    def proj(lo, hi):
        return jnp.dot(h, w_ref[:, lo:hi], preferred_element_type=F32)

    q_ref[...] = proj(0, 512).astype(BF16)
    k_ref[...] = proj(512, 1024).astype(BF16)
    v_ref[...] = proj(1024, 1536).astype(BF16)
    z_ref[...] = proj(1536, 2048)
    xbc_ref[...] = proj(2048, 3072)
    dt_ref[...] = jnp.dot(h, wdt_ref[...], preferred_element_type=F32)


def _inproj(x, nw, w_main, w_dt):
    n = x.shape[0]
    tm = IN_TM
    row = lambda i: (i, 0)
    full = lambda i: (0, 0)
    return pl.pallas_call(
        _inproj_kernel,
        grid=(n // tm,),
        in_specs=[
            pl.BlockSpec((tm, D_MODEL), row),
            pl.BlockSpec((1, D_MODEL), full),
            pl.BlockSpec((D_MODEL, MAIN_COLS), full),
            pl.BlockSpec((D_MODEL, LANES), full),
        ],
        out_specs=[
            pl.BlockSpec((tm, 512), row),
            pl.BlockSpec((tm, 512), row),
            pl.BlockSpec((tm, 512), row),
            pl.BlockSpec((tm, 512), row),
            pl.BlockSpec((tm, 1024), row),
            pl.BlockSpec((tm, LANES), row),
        ],
        out_shape=[
            jax.ShapeDtypeStruct((n, 512), BF16),
            jax.ShapeDtypeStruct((n, 512), BF16),
            jax.ShapeDtypeStruct((n, 512), BF16),
            jax.ShapeDtypeStruct((n, 512), F32),
            jax.ShapeDtypeStruct((n, 1024), F32),
            jax.ShapeDtypeStruct((n, LANES), F32),
        ],
        compiler_params=_cparams(("parallel",)),
        name="inproj",
    )(x, nw, w_main, w_dt)


ATT_RB = 8
ATT_TOK = ATT_RB * GRID_W
ATT_KEYS = WIN_H * GRID_W


def _attn_kernel(q_ref, kp_ref, kc_ref, kn_ref, vp_ref, vc_ref, vn_ref, bias_ref, o_ref,
                 kbuf, vbuf, *, rows):
    rb = pl.program_id(2)
    kbuf[0:ATT_TOK] = kp_ref[...]
    kbuf[ATT_TOK:2 * ATT_TOK] = kc_ref[...]
    kbuf[2 * ATT_TOK:3 * ATT_TOK] = kn_ref[...]
    vbuf[0:ATT_TOK] = vp_ref[...]
    vbuf[ATT_TOK:2 * ATT_TOK] = vc_ref[...]
    vbuf[2 * ATT_TOK:3 * ATT_TOK] = vn_ref[...]
    lane = lax.broadcasted_iota(jnp.int32, (GRID_W, LANES), 1)
    for r in range(ATT_RB):
        grow = rb * ATT_RB + r
        row_start = jnp.clip(grow - WIN_H // 2, 0, rows - WIN_H)
        off = pl.multiple_of((row_start - (rb - 1) * ATT_RB) * GRID_W, GRID_W)
        var = row_start - grow + (WIN_H - 1)
        kw = kbuf[pl.ds(off, ATT_KEYS), :]
        vw = vbuf[pl.ds(off, ATT_KEYS), :]
        q = q_ref[r * GRID_W:(r + 1) * GRID_W, :]
        acc = jnp.zeros((GRID_W, LANES), F32)
        for hh in range(2):
            inh = (lane >= hh * ATT_HEAD_DIM) & (lane < (hh + 1) * ATT_HEAD_DIM)
            qm = jnp.where(inh, q, jnp.zeros_like(q))
            s = lax.dot_general(qm, kw, NT_DIMS, preferred_element_type=F32)
            s = s * (ATT_HEAD_DIM ** -0.5) + bias_ref[hh, var]
            m = jnp.max(s, axis=-1, keepdims=True)
            p = jnp.exp(s - m)
            l = jnp.sum(p, axis=-1, keepdims=True)
            o = jnp.dot(p.astype(BF16), vw, preferred_element_type=F32) / l
            acc = jnp.where(inh, o, acc)
        o_ref[r * GRID_W:(r + 1) * GRID_W, :] = acc.astype(BF16)


def _attn_bias(rpb):
    c = jnp.arange(GRID_W)
    col_start = jnp.clip(c - WIN_W // 2, 0, GRID_W - WIN_W)
    col_in = (c[None, :] >= col_start[:, None]) & (c[None, :] < col_start[:, None] + WIN_W)
    dc = jnp.clip(c[None, :] - c[:, None], -(WIN_W - 1), WIN_W - 1) + (WIN_W - 1)
    t = rpb.astype(F32)[:, :, dc]
    t = jnp.where(col_in[None, None], t, NEG_BIG)
    idx = jnp.arange(WIN_H)[:, None] + jnp.arange(WIN_H)[None, :]
    t = t[:, idx]
    t = jnp.transpose(t, (0, 1, 3, 2, 4))
    return t.reshape(ATT_HEADS, WIN_H, GRID_W, ATT_KEYS)


def _attention(q, k, v, bias):
    b, l, _ = q.shape
    rows = l // GRID_W
    assert rows >= 2 * WIN_H and rows % ATT_RB == 0
    nrb = rows // ATT_RB
    cur = lambda hp, bi, rb: (bi, rb, hp)
    prv = lambda hp, bi, rb: (bi, jnp.maximum(rb - 1, 0), hp)
    nxt = lambda hp, bi, rb: (bi, jnp.minimum(rb + 1, nrb - 1), hp)
    blk = (None, ATT_TOK, LANES)
    return pl.pallas_call(
        functools.partial(_attn_kernel, rows=rows),
        grid=(ATT_HEADS // 2, b, nrb),
        in_specs=[
            pl.BlockSpec(blk, cur),
            pl.BlockSpec(blk, prv), pl.BlockSpec(blk, cur), pl.BlockSpec(blk, nxt),
            pl.BlockSpec(blk, prv), pl.BlockSpec(blk, cur), pl.BlockSpec(blk, nxt),
            pl.BlockSpec((2, WIN_H, GRID_W, ATT_KEYS), lambda hp, bi, rb: (hp, 0, 0, 0)),
        ],
        out_specs=pl.BlockSpec(blk, cur),
        out_shape=jax.ShapeDtypeStruct((b, l, ATT_WIDTH), BF16),
        scratch_shapes=[pltpu.VMEM((3 * ATT_TOK, LANES), BF16),
                        pltpu.VMEM((3 * ATT_TOK, LANES), BF16)],
        compiler_params=_cparams(("parallel", "parallel", "parallel")),
        name="nbr_attention",
    )(q, k, k, k, v, v, v, bias)


CONV_TL = 512
HALO = 8


def _conv_kernel(xp_ref, xc_ref, xn_ref, w_ref, b_ref, o_ref, buf):
    i = pl.program_id(1)
    nblk = pl.num_programs(1)
    prev = xp_ref[...]
    nxt = xn_ref[...]
    buf[0:HALO] = jnp.where(i == 0, jnp.zeros_like(prev), prev)
    buf[HALO:HALO + CONV_TL] = xc_ref[...]
    buf[HALO + CONV_TL:2 * HALO + CONV_TL] = jnp.where(i == nblk - 1, jnp.zeros_like(nxt), nxt)
    pad = SSD_CONV // 2
    y = jnp.zeros((CONV_TL, CONV_CH), F32) + b_ref[...]
    for j in range(SSD_CONV):
        y = y + buf[HALO - pad + j:HALO - pad + j + CONV_TL, :] * w_ref[j:j + 1, :]
    o_ref[...] = y * (1.0 / (1.0 + jnp.exp(-y)))


def _conv_silu(xbc, conv_w, conv_b):
    b, l, ch = xbc.shape
    tl = CONV_TL
    nblk = l // tl
    per = tl // HALO
    nh = l // HALO
    return pl.pallas_call(
        _conv_kernel,
        grid=(b, nblk),
        in_specs=[
            pl.BlockSpec((None, HALO, ch), lambda bi, i: (bi, jnp.maximum(i * per - 1, 0), 0)),
            pl.BlockSpec((None, tl, ch), lambda bi, i: (bi, i, 0)),
            pl.BlockSpec((None, HALO, ch), lambda bi, i: (bi, jnp.minimum((i + 1) * per, nh - 1), 0)),
            pl.BlockSpec((SSD_CONV, ch), lambda bi, i: (0, 0)),
            pl.BlockSpec((1, ch), lambda bi, i: (0, 0)),
        ],
        out_specs=pl.BlockSpec((None, tl, ch), lambda bi, i: (bi, i, 0)),
        out_shape=jax.ShapeDtypeStruct((b, l, ch), F32),
        scratch_shapes=[pltpu.VMEM((tl + 2 * HALO, ch), F32)],
        compiler_params=_cparams(("parallel", "parallel")),
        name="conv_silu",
    )(xbc, xbc, xbc, conv_w, conv_b)


Q = SSD_CHUNK


def _ssd_direction(xs_ref, bm_ref, cm_ref, dt_ref, bias, a_vec, tri, expand_m, y_ref, s_ref, fwd):
    xx = dt_ref[...] + bias
    dtv = jnp.maximum(xx, 0.0) + jnp.log1p(jnp.exp(-jnp.abs(xx)))
    adt = dtv * a_vec
    hi, mid, lo = _split3(adt)
    cs3 = jnp.dot(tri, jnp.concatenate([hi, mid, lo], axis=1), preferred_element_type=F32)
    cs = cs3[:, 0:LANES] + cs3[:, LANES:2 * LANES] + cs3[:, 2 * LANES:3 * LANES]

    def expand(v):
        h3, m3, l3 = _split3(v)
        r = jnp.dot(jnp.concatenate([h3, m3, l3], axis=0), expand_m, preferred_element_type=F32)
        return r[0:Q] + r[Q:2 * Q] + r[2 * Q:3 * Q]

    dt_x = expand(dtv)
    cs_x = expand(cs)
    last = Q - 1 if fwd else 0
    tot_x = cs_x[last:last + 1, :]
    xdt = xs_ref[...] * dt_x
    xdt_b = xdt.astype(BF16)
    xw = (xdt * jnp.exp(tot_x - cs_x)).astype(BF16)
    scale_off = jnp.exp(cs_x)
    chunk_decay = jnp.exp(tot_x)
    cs_t = cs.T
    row = lax.broadcasted_iota(jnp.int32, (Q, Q), 0)
    col = lax.broadcasted_iota(jnp.int32, (Q, Q), 1)
    tri_mask = (col <= row) if fwd else (col >= row)
    gw = SSD_WIDTH // SSD_GROUPS
    for g in range(SSD_GROUPS):
        bg = bm_ref[:, g * SSD_STATE:(g + 1) * SSD_STATE]
        cg = cm_ref[:, g * SSD_STATE:(g + 1) * SSD_STATE].astype(BF16)
        cb = lax.dot_general(cg, bg.astype(BF16), NT_DIMS, preferred_element_type=F32)
        state = s_ref[g]
        yoff = jnp.dot(cg, state.astype(BF16), preferred_element_type=F32) * scale_off[:, g * gw:(g + 1) * gw]
        bg_t = bg.T.astype(BF16)
        s_ref[g] = state * chunk_decay[:, g * gw:(g + 1) * gw] + jnp.dot(
            bg_t, xw[:, g * gw:(g + 1) * gw], preferred_element_type=F32)
        for pp in range(2):
            pair = 2 * g + pp
            xpair = xdt_b[:, pair * LANES:(pair + 1) * LANES]
            ypair = yoff[:, pp * LANES:(pp + 1) * LANES]
            for hh in range(2):
                hc = 2 * pair + hh + (0 if fwd else SSD_HEADS)
                diff = cs[:, hc:hc + 1] - cs_t[hc:hc + 1, :]
                lm = jnp.where(tri_mask, jnp.exp(jnp.minimum(diff, 0.0)), 0.0)
                mm = (cb * lm).astype(BF16)
                inh = (col >= hh * SSD_HEAD_DIM) & (col < (hh + 1) * SSD_HEAD_DIM)
                ypair = ypair + jnp.dot(mm, jnp.where(inh, xpair, jnp.zeros_like(xpair)),
                                        preferred_element_type=F32)
            y_ref[:, pair * LANES:(pair + 1) * LANES] = ypair


def _ssd_kernel(xs_f, bm_f, cm_f, dt_f, xs_b, bm_b, cm_b, dt_b, bias_ref, alog_ref, tri_ref, e_ref,
                yf_ref, yb_ref, sf_ref, sb_ref):
    @pl.when(pl.program_id(1) == 0)
    def _():
        sf_ref[...] = jnp.zeros_like(sf_ref)
        sb_ref[...] = jnp.zeros_like(sb_ref)

    lane = lax.broadcasted_iota(jnp.int32, (1, LANES), 1)
    bias = bias_ref[...]
    a_vec = jnp.where(lane < 2 * SSD_HEADS, -jnp.exp(alog_ref[...]), 0.0)
    _ssd_direction(xs_f, bm_f, cm_f, dt_f, bias, a_vec, tri_ref[0], e_ref[0], yf_ref, sf_ref, True)
    _ssd_direction(xs_b, bm_b, cm_b, dt_b, bias, a_vec, tri_ref[1], e_ref[1], yb_ref, sb_ref, False)


def _ssd_scan(xbc_act, dt, bias_vec, alog_vec, tri, expand_m):
    b, l, _ = xbc_act.shape
    nc = l // Q
    fw = lambda cb: (lambda bi, c: (bi, c, cb))
    bw = lambda cb: (lambda bi, c: (bi, nc - 1 - c, cb))
    const2 = lambda bi, c: (0, 0)
    const3 = lambda bi, c: (0, 0, 0)

    def chunk_specs(mk):
        return [
            pl.BlockSpec((None, Q, SSD_WIDTH), mk(0)),
            pl.BlockSpec((None, Q, 2 * SSD_STATE), mk(2)),
            pl.BlockSpec((None, Q, 2 * SSD_STATE), mk(3)),
            pl.BlockSpec((None, Q, LANES), mk(0)),
        ]

    return pl.pallas_call(
        _ssd_kernel,
        grid=(b, nc),
        in_specs=chunk_specs(fw) + chunk_specs(bw) + [
            pl.BlockSpec((1, LANES), const2),
            pl.BlockSpec((1, LANES), const2),
            pl.BlockSpec((2, Q, Q), const3),
            pl.BlockSpec((2, LANES, SSD_WIDTH), const3),
        ],
        out_specs=[
            pl.BlockSpec((None, Q, SSD_WIDTH), fw(0)),
            pl.BlockSpec((None, Q, SSD_WIDTH), bw(0)),
        ],
        out_shape=[jax.ShapeDtypeStruct((b, l, SSD_WIDTH), F32)] * 2,
        scratch_shapes=[pltpu.VMEM((SSD_GROUPS, SSD_STATE, 256), F32),
                        pltpu.VMEM((SSD_GROUPS, SSD_STATE, 256), F32)],
        compiler_params=_cparams(("parallel", "arbitrary")),
        name="ssd_scan",
    )(xbc_act, xbc_act, xbc_act, dt, xbc_act, xbc_act, xbc_act, dt, bias_vec, alog_vec, tri, expand_m)


OUT_TM = 512


def _outproj_kernel(x_ref, att_ref, xs_ref, yf_ref, yb_ref, z_ref, dx_ref, snw_ref, wo_ref, x1_ref):
    z = z_ref[...]
    y = xs_ref[...] * dx_ref[...] + yf_ref[...] + yb_ref[...]
    y = y * (z * (1.0 / (1.0 + jnp.exp(-z))))
    ms = jnp.mean(y * y, axis=-1, keepdims=True)
    ssd = (y * lax.rsqrt(ms + EPS) * snw_ref[...]).astype(BF16)
    mixed = jnp.dot(att_ref[...], wo_ref[0:ATT_WIDTH, :], preferred_element_type=F32)
    mixed = mixed + jnp.dot(ssd, wo_ref[ATT_WIDTH:, :], preferred_element_type=F32)
    x1_ref[...] = x_ref[...] + mixed


def _outproj(x, att, xbc_act, yf, yb, z, d_x, snw, w_out):
    n = x.shape[0]
    tm = OUT_TM
    row = lambda i: (i, 0)
    full = lambda i: (0, 0)
    return pl.pallas_call(
        _outproj_kernel,
        grid=(n // tm,),
        in_specs=[
            pl.BlockSpec((tm, D_MODEL), row),
            pl.BlockSpec((tm, ATT_WIDTH), row),
            pl.BlockSpec((tm, SSD_WIDTH), row),
            pl.BlockSpec((tm, SSD_WIDTH), row),
            pl.BlockSpec((tm, SSD_WIDTH), row),
            pl.BlockSpec((tm, SSD_WIDTH), row),
            pl.BlockSpec((1, SSD_WIDTH), full),
            pl.BlockSpec((1, SSD_WIDTH), full),
            pl.BlockSpec((D_MODEL, D_MODEL), full),
        ],
        out_specs=pl.BlockSpec((tm, D_MODEL), row),
        out_shape=jax.ShapeDtypeStruct((n, D_MODEL), F32),
        compiler_params=_cparams(("parallel",)),
        name="outproj",
    )(x, att, xbc_act, yf, yb, z, d_x, snw, w_out)


PEER_TB = 1024
PEER_PIECE = 256
PEER_EC = 1024
PEER_M = PEER_EC // PEER_N_KEYS
BF16_ROWS = 16
N_JT = PEER_N_KEYS // BF16_ROWS


SUBLANES = 8
N_ROWV = PEER_N_KEYS // SUBLANES


def _oddeven_merge_sort_pairs(n):
    out, p = [], 1
    while p < n:
        k = p
        while k >= 1:
            for j in range(k % p, n - k, 2 * k):
                for i in range(min(k, n - j - k)):
                    if (i + j) // (2 * p) == (i + j + k) // (2 * p):
                        out.append((i + j, i + j + k))
            k //= 2
        p *= 2
    return out


SORT16_PAIRS = _oddeven_merge_sort_pairs(PEER_TOPK)


def _compare_exchange(rows, i, j):
    a, b = rows[i], rows[j]
    rows[i] = jnp.maximum(a, b)
    rows[j] = jnp.minimum(a, b)


def _sort_desc(rows):
    for i, j in SORT16_PAIRS:
        if j < len(rows):
            _compare_exchange(rows, i, j)


def _top16_over_sublanes(rows):
    rows = list(rows) + [None] * (PEER_TOPK - len(rows))
    for shift in (4, 2, 1):
        other = [None if r is None else pltpu.roll(r, shift, 0) for r in rows]
        merged = []
        for k in range(PEER_TOPK):
            a, b = rows[k], other[PEER_TOPK - 1 - k]
            merged.append(b if a is None else a if b is None else jnp.maximum(a, b))
        rows = merged
        for stride in (8, 4, 2, 1):
            for k in range(PEER_TOPK):
                if k & stride == 0:
                    _compare_exchange(rows, k, k + stride)
    return rows


def _sum_over_sublanes(x):
    for shift in (4, 2, 1):
        x = x + pltpu.roll(x, shift, 0)
    return x


def _peer_select_tile(s0_s, s1_s, lanes, h, r1_s, e1_s, n_s, c_s):
    rowsl = lambda ref, k: ref[k * SUBLANES:(k + 1) * SUBLANES, lanes]
    s0 = [rowsl(s0_s, k) for k in range(N_ROWV)]
    s1 = [rowsl(s1_s, k) for k in range(N_ROWV)]
    v0, v1 = list(s0), list(s1)
    _sort_desc(v0)
    _sort_desc(v1)
    v0 = _top16_over_sublanes(v0)
    v1 = _top16_over_sublanes(v1)
    sub = lax.broadcasted_iota(jnp.int32, (SUBLANES, LANES), 0)

    def pack(vals):
        out = vals[0]
        for s in range(1, SUBLANES):
            out = jnp.where(sub == s, vals[s], out)
        return out

    v1_lo, v1_hi, v0_hi = pack(v1[0:8]), pack(v1[8:16]), pack(v0[8:16])
    cand = [v0[0] + v1_lo, v0[0] + v1_hi] + [v0[a] + v1_lo for a in range(1, 8)] + [v0_hi + v1[0]]
    csort = list(cand)
    _sort_desc(csort)
    csort = _top16_over_sublanes(csort)
    tau, cmax = csort[PEER_TOPK - 1], csort[0]
    zsum = jnp.zeros_like(tau)
    for cv in cand:
        zsum = zsum + jnp.where(cv >= tau, jnp.exp(cv - cmax), 0.0)
    rz = 1.0 / _sum_over_sublanes(zsum)
    n_a = []
    for a in range(PEER_TOPK):
        cnt = jnp.where(v0[a] + v1_lo >= tau, 1.0, 0.0) + jnp.where(v0[a] + v1_hi >= tau, 1.0, 0.0)
        n_a.append(_sum_over_sublanes(cnt))
    for m in range(N_JT):
        packed_r, packed_e = [], []
        for k in (2 * m, 2 * m + 1):
            n = jnp.zeros_like(tau)
            r = jnp.full_like(tau, float(PEER_TOPK))
            for a in range(PEER_TOPK):
                n = jnp.where(s0[k] == v0[a], n_a[a], n)
                r = jnp.where(s1[k] == v1[a], float(a), r)
            n_s[h, k, :, lanes] = n
            c_s[h, k, :, lanes] = jnp.exp(s0[k] - v0[0]) * rz
            packed_r.append(r)
            packed_e.append(jnp.exp(s1[k] - v1[0]))
        r1_s[h, m, :, lanes] = jnp.concatenate(packed_r, axis=0).astype(BF16)
        e1_s[h, m, :, lanes] = jnp.concatenate(packed_e, axis=0).astype(BF16)


def _peer_kernel(x1_ref, fw_ref, wqt_ref, sk_ref, u_ref, vt_ref, nw_ref, o_ref,
                 hf_s, qt_s, r1_s, e1_s, n_s, c_s, s0_s, s1_s, acc_s):
    c = pl.program_id(1)

    @pl.when(c == 0)
    def _prepare():
        x1 = x1_ref[...]
        ms = jnp.mean(x1 * x1, axis=-1, keepdims=True)
        hf_s[...] = (x1 * lax.rsqrt(ms + EPS) * fw_ref[...]).T.astype(BF16)
        qt_s[...] = jnp.dot(wqt_ref[...], hf_s[...], preferred_element_type=F32).astype(BF16)
        acc_s[...] = jnp.zeros_like(acc_s)

        def per_head(h, carry):
            r0 = pl.multiple_of(h * 2 * PEER_D_KEY, 2 * PEER_D_KEY)
            s0 = jnp.dot(sk_ref[2 * h], qt_s[pl.ds(r0, PEER_D_KEY), :], preferred_element_type=F32)
            s1 = jnp.dot(sk_ref[2 * h + 1], qt_s[pl.ds(r0 + PEER_D_KEY, PEER_D_KEY), :],
                         preferred_element_type=F32)
            s0_s[...] = s0
            s1_s[...] = s1

            def per_tile(lt, carry2):
                lanes = pl.ds(pl.multiple_of(lt * LANES, LANES), LANES)
                _peer_select_tile(s0_s, s1_s, lanes, h, r1_s, e1_s, n_s, c_s)
                return carry2

            lax.fori_loop(0, PEER_TB // LANES, per_tile, 0)
            return carry

        lax.fori_loop(0, PEER_HEADS, per_head, 0)

    n_pieces = PEER_TB // PEER_PIECE

    def expert_scores(p):
        return jnp.dot(u_ref[...], hf_s[:, p * PEER_PIECE:(p + 1) * PEER_PIECE],
                       preferred_element_type=F32)

    xu_next = expert_scores(0)
    for p in range(n_pieces):
        lanes = slice(p * PEER_PIECE, (p + 1) * PEER_PIECE)
        xu = xu_next
        if p + 1 < n_pieces:
            xu_next = expert_scores(p + 1)
        xb = xu.astype(BF16)
        act = (0.5 * xb) * (1.0 + lax.erf(xb * (2.0 ** -0.5)))
        pieces = []
        for ii in range(PEER_M):
            g = jnp.zeros((N_JT, BF16_ROWS, PEER_PIECE), BF16)
            for h in range(PEER_HEADS):
                nb = jnp.broadcast_to(n_s[h, c, ii:ii + 1, lanes], (BF16_ROWS, PEER_PIECE)).astype(BF16)
                cb = jnp.broadcast_to(c_s[h, c, ii:ii + 1, lanes], (BF16_ROWS, PEER_PIECE)).astype(BF16)
                sel = jnp.where(r1_s[h, :, :, lanes] < nb[None], e1_s[h, :, :, lanes], jnp.zeros((), BF16))
                g = g + sel * cb[None]
            pieces.append(act[ii * PEER_N_KEYS:(ii + 1) * PEER_N_KEYS, :] * g.reshape(PEER_N_KEYS, PEER_PIECE))
        a_t = jnp.concatenate(pieces, axis=0)
        acc_s[:, lanes] += jnp.dot(vt_ref[...], a_t, preferred_element_type=F32)

    @pl.when(c == pl.num_programs(1) - 1)
    def _finish():
        y = x1_ref[...] + acc_s[...].T
        ms = jnp.mean(y * y, axis=-1, keepdims=True)
        o_ref[...] = y * lax.rsqrt(ms + EPS) * nw_ref[...]


def _peer(x1, fw, wq_t, sub_keys, exp_u, exp_vt, nw):
    n = x1.shape[0]
    tb, ec = PEER_TB, PEER_EC
    n_chunks = PEER_N_EXPERTS // ec
    tok = lambda t, c: (t, 0)
    const2 = lambda t, c: (0, 0)
    head_tiles = (PEER_HEADS, N_JT, BF16_ROWS, tb)
    key_tiles = (PEER_HEADS, N_ROWV, SUBLANES, tb)
    assert PEER_M == SUBLANES
    return pl.pallas_call(
        _peer_kernel,
        grid=(n // tb, n_chunks),
        in_specs=[
            pl.BlockSpec((tb, D_MODEL), tok, pipeline_mode=pl.Buffered(1)),
            pl.BlockSpec((1, D_MODEL), const2),
            pl.BlockSpec((2 * PEER_HEADS * PEER_D_KEY, D_MODEL), const2, pipeline_mode=pl.Buffered(1)),
            pl.BlockSpec((2 * PEER_HEADS, PEER_N_KEYS, PEER_D_KEY), lambda t, c: (0, 0, 0)),
            pl.BlockSpec((ec, D_MODEL), lambda t, c: (c, 0)),
            pl.BlockSpec((D_MODEL, ec), lambda t, c: (0, c)),
            pl.BlockSpec((1, D_MODEL), const2),
        ],
        out_specs=pl.BlockSpec((tb, D_MODEL), tok),
        out_shape=jax.ShapeDtypeStruct((n, D_MODEL), F32),
        scratch_shapes=[
            pltpu.VMEM((D_MODEL, tb), BF16),
            pltpu.VMEM((2 * PEER_HEADS * PEER_D_KEY, tb), BF16),
            pltpu.VMEM(head_tiles, BF16),
            pltpu.VMEM(head_tiles, BF16),
            pltpu.VMEM(key_tiles, F32),
            pltpu.VMEM(key_tiles, F32),
            pltpu.VMEM((PEER_N_KEYS, tb), F32),
            pltpu.VMEM((PEER_N_KEYS, tb), F32),
            pltpu.VMEM((D_MODEL, tb), F32),
        ],
        compiler_params=_cparams(("parallel", "arbitrary")),
        name="peer",
    )(x1, fw, wq_t, sub_keys, exp_u, exp_vt, nw)


def _lane_vec(two_by_heads):
    v = two_by_heads.astype(F32).reshape(1, 2 * SSD_HEADS)
    return jnp.pad(v, ((0, 0), (0, LANES - 2 * SSD_HEADS)))


def _trunk(x, p):
    b, l, d = x.shape
    n = b * l
    xf = x.reshape(n, d)
    q, k, v, z, xbc, dt = _inproj(xf, p["norm_mix_w"], p["w_main"], p["w_dt"])
    att = _attention(q.reshape(b, l, -1), k.reshape(b, l, -1), v.reshape(b, l, -1), p["att_bias"])
    xbc_act = _conv_silu(xbc.reshape(b, l, -1), p["conv_w"], p["conv_b"])
    yf, yb = _ssd_scan(xbc_act, dt.reshape(b, l, -1), p["dt_bias_vec"], p["a_log_vec"], p["tri"], p["expand"])
    x1 = _outproj(xf, att.reshape(n, -1), xbc_act.reshape(n, -1), yf.reshape(n, -1), yb.reshape(n, -1),
                  z, p["d_x"], p["ssd_norm_w"], p["w_out"])
    y = _peer(x1, p["norm_ffn_w"], p["wq_t"], p["sub_keys"], p["exp_u"], p["exp_vt"], p["norm_final_w"])
    return y.reshape(b, l, d)


def kernel(x_prompt, x_sample, norm_mix_w, w_in, rpb, conv_w, conv_b, dt_bias, a_log, d_skip, ssd_norm_w,
           w_out, norm_ffn_w, w_query, sub_keys, expert_u, expert_v, norm_final_w):
    assert w_in.shape[0] == 1, "single layer"
    w_in0 = w_in[0]
    w_dt = w_in0[:, MAIN_COLS:]
    w_dt = jnp.concatenate([w_dt, w_dt, jnp.zeros((D_MODEL, LANES - 2 * SSD_HEADS), w_dt.dtype)], axis=1)
    r = jnp.arange(Q)
    tril = (r[None, :] <= r[:, None])
    tri = jnp.stack([tril, tril.T]).astype(BF16)
    hid = jnp.arange(SSD_WIDTH) // SSD_HEAD_DIM
    lane = jnp.arange(LANES)
    expand = jnp.stack([lane[:, None] == hid[None, :],
                        lane[:, None] == hid[None, :] + SSD_HEADS]).astype(BF16)
    p = {
        "norm_mix_w": norm_mix_w[0].reshape(1, -1),
        "w_main": w_in0[:, :MAIN_COLS].astype(BF16),
        "w_dt": w_dt.astype(BF16),
        "att_bias": _attn_bias(rpb[0]),
        "conv_w": conv_w[0],
        "conv_b": conv_b[0].reshape(1, -1),
        "dt_bias_vec": _lane_vec(dt_bias[0]),
        "a_log_vec": _lane_vec(a_log[0]),
        "tri": tri,
        "expand": expand,
        "d_x": jnp.repeat(d_skip[0].astype(F32), SSD_HEAD_DIM).reshape(1, -1),
        "ssd_norm_w": ssd_norm_w[0].reshape(1, -1),
        "w_out": w_out[0].astype(BF16),
        "norm_ffn_w": norm_ffn_w[0].reshape(1, -1),
        "wq_t": w_query[0].T.astype(BF16),
        "sub_keys": sub_keys[0].reshape(2 * PEER_HEADS, PEER_N_KEYS, PEER_D_KEY).astype(BF16),
        "exp_u": expert_u[0].astype(BF16),
        "exp_vt": expert_v[0].T.astype(BF16),
        "norm_final_w": norm_final_w.reshape(1, -1),
    }
    return (_trunk(x_prompt, p), _trunk(x_sample, p))
```

```python
import functools
import math

import jax
import jax.numpy as jnp
from jax import lax
from jax.experimental import pallas as pl
from jax.experimental.pallas import tpu as pltpu

F32 = jnp.float32
BF16 = jnp.bfloat16

D_MODEL = 1024
GRID_W = 64
ATT_HEADS = 8
ATT_HEAD_DIM = 64
ATT_WIDTH = 512
WIN_H = 8
WIN_W = 16
SSD_HEADS = 8
SSD_HEAD_DIM = 64
SSD_WIDTH = 512
SSD_GROUPS = 2
SSD_STATE = 128
SSD_CONV = 5
SSD_CHUNK = 128
CONV_CH = 1024
MAIN_COLS = 3 * ATT_WIDTH + SSD_WIDTH + CONV_CH
PEER_HEADS = 8
PEER_N_KEYS = 128
PEER_N_EXPERTS = PEER_N_KEYS * PEER_N_KEYS
PEER_D_KEY = 128
PEER_TOPK = 16
EPS = 1e-6

LANES = 128
VMEM_LIMIT = 60 * 1024 * 1024
NEG_BIG = -1e30
NT_DIMS = (((1,), (1,)), ((), ()))


def _cparams(sem):
    return pltpu.CompilerParams(dimension_semantics=sem, vmem_limit_bytes=VMEM_LIMIT)


def _split3(v):
    hi = v.astype(BF16)
    r1 = v - hi.astype(F32)
    mid = r1.astype(BF16)
    lo = (r1 - mid.astype(F32)).astype(BF16)
    return hi, mid, lo


IN_TM = 512


def _inproj_kernel(x_ref, nw_ref, w_ref, wdt_ref, q_ref, k_ref, v_ref, z_ref, xbc_ref, dt_ref):
    x = x_ref[...]
    ms = jnp.mean(x * x, axis=-1, keepdims=True)
    h = (x * lax.rsqrt(ms + EPS) * nw_ref[...]).astype(BF16)

    def proj(lo, hi):
        return jnp.dot(h, w_ref[:, lo:hi], preferred_element_type=F32)

    q_ref[...] = proj(0, 512).astype(BF16)
    k_ref[...] = proj(512, 1024).astype(BF16)
    v_ref[...] = proj(1024, 1536).astype(BF16)
    z_ref[...] = proj(1536, 2048)
    xbc_ref[...] = proj(2048, 3072)
    dt_ref[...] = jnp.dot(h, wdt_ref[...], preferred_element_type=F32)


def _inproj(x, nw, w_main, w_dt):
    n = x.shape[0]
    tm = IN_TM
    row = lambda i: (i, 0)
    full = lambda i: (0, 0)
    return pl.pallas_call(
        _inproj_kernel,
        grid=(n // tm,),
        in_specs=[
            pl.BlockSpec((tm, D_MODEL), row),
            pl.BlockSpec((1, D_MODEL), full),
            pl.BlockSpec((D_MODEL, MAIN_COLS), full),
            pl.BlockSpec((D_MODEL, LANES), full),
        ],
        out_specs=[
            pl.BlockSpec((tm, 512), row),
            pl.BlockSpec((tm, 512), row),
            pl.BlockSpec((tm, 512), row),
            pl.BlockSpec((tm, 512), row),
            pl.BlockSpec((tm, 1024), row),
            pl.BlockSpec((tm, LANES), row),
        ],
        out_shape=[
            jax.ShapeDtypeStruct((n, 512), BF16),
            jax.ShapeDtypeStruct((n, 512), BF16),
            jax.ShapeDtypeStruct((n, 512), BF16),
            jax.ShapeDtypeStruct((n, 512), F32),
            jax.ShapeDtypeStruct((n, 1024), F32),
            jax.ShapeDtypeStruct((n, LANES), F32),
        ],
        compiler_params=_cparams(("parallel",)),
        name="inproj",
    )(x, nw, w_main, w_dt)


ATT_RB = 8
ATT_TOK = ATT_RB * GRID_W
ATT_HALO = WIN_H // 2
ATT_UROWS = ATT_RB + 2 * ATT_HALO
ATT_KEYS = ATT_UROWS * GRID_W
ATT_VARIANTS = 3
ATT_SUB = 4
ATT_SUB_TOK = ATT_SUB * GRID_W
ATT_SUB_KEYS = (ATT_SUB + 2 * ATT_HALO) * GRID_W


def _attn_kernel(q_ref, kp_ref, kc_ref, kn_ref, vp_ref, vc_ref, vn_ref, bias_ref, o_ref):
    rb = pl.program_id(2)
    nrb = pl.num_programs(2)
    variant = jnp.where(rb == 0, 0, jnp.where(rb == nrb - 1, 2, 1))
    halo = ATT_HALO * GRID_W
    kw = jnp.concatenate([kp_ref[ATT_TOK - halo:, :], kc_ref[...], kn_ref[0:halo, :]], axis=0)
    vw = jnp.concatenate([vp_ref[ATT_TOK - halo:, :], vc_ref[...], vn_ref[0:halo, :]], axis=0)
    lane = lax.broadcasted_iota(jnp.int32, (ATT_SUB_TOK, LANES), 1)
    for sb in range(ATT_RB // ATT_SUB):
        q = q_ref[sb * ATT_SUB_TOK:(sb + 1) * ATT_SUB_TOK, :]
        ks = kw[sb * ATT_SUB_TOK:sb * ATT_SUB_TOK + ATT_SUB_KEYS, :]
        vs = vw[sb * ATT_SUB_TOK:sb * ATT_SUB_TOK + ATT_SUB_KEYS, :]
        acc = jnp.zeros((ATT_SUB_TOK, LANES), F32)
        for hh in range(2):
            inh = (lane >= hh * ATT_HEAD_DIM) & (lane < (hh + 1) * ATT_HEAD_DIM)
            qm = jnp.where(inh, q, jnp.zeros_like(q))
            s = lax.dot_general(qm, ks, NT_DIMS, preferred_element_type=F32)
            s = s * (ATT_HEAD_DIM ** -0.5) + bias_ref[hh, variant, sb]
            m = jnp.max(s, axis=-1, keepdims=True)
            p = jnp.exp(s - m)
            l = jnp.sum(p, axis=-1, keepdims=True)
            o = jnp.dot(p.astype(BF16), vs, preferred_element_type=F32) / l
            acc = jnp.where(inh, o, acc)
        o_ref[sb * ATT_SUB_TOK:(sb + 1) * ATT_SUB_TOK, :] = acc.astype(BF16)


def _attn_bias(rpb):
    c = jnp.arange(GRID_W)
    col_start = jnp.clip(c - WIN_W // 2, 0, GRID_W - WIN_W)
    col_in = (c[None, :] >= col_start[:, None]) & (c[None, :] < col_start[:, None] + WIN_W)
    dc = jnp.clip(c[None, :] - c[:, None], -(WIN_W - 1), WIN_W - 1) + (WIN_W - 1)
    t = rpb.astype(F32)[:, :, dc]
    t = jnp.where(col_in[None, None], t, NEG_BIG)
    n_sub = ATT_RB // ATT_SUB
    sub_rows = ATT_SUB + 2 * ATT_HALO
    qr = (jnp.arange(n_sub)[:, None, None] * ATT_SUB + jnp.arange(ATT_SUB)[None, :, None])
    u = jnp.arange(n_sub)[:, None, None] * ATT_SUB + jnp.arange(sub_rows)[None, None, :]
    dr = jnp.clip(u - ATT_HALO - qr + (WIN_H - 1), 0, 2 * WIN_H - 2)
    first = jnp.maximum(qr - ATT_HALO, 0) + ATT_HALO
    last = jnp.minimum(qr - ATT_HALO, 0) + ATT_HALO
    starts = jnp.stack([jnp.broadcast_to(s, dr.shape) for s in (first, qr, last)])
    valid = (u[None] >= starts) & (u[None] < starts + WIN_H)
    t = t[:, dr]
    t = jnp.where(valid[None, :, :, :, :, None, None], t[:, None], NEG_BIG)
    t = jnp.transpose(t, (0, 1, 2, 3, 5, 4, 6))
    return t.reshape(ATT_HEADS, ATT_VARIANTS, n_sub, ATT_SUB_TOK, ATT_SUB_KEYS)


def _attention(q, k, v, bias):
    b, l, _ = q.shape
    rows = l // GRID_W
    assert rows % ATT_RB == 0 and rows // ATT_RB >= 2
    nrb = rows // ATT_RB
    cur = lambda hp, bi, rb: (bi, rb, hp)
    prv = lambda hp, bi, rb: (bi, jnp.maximum(rb - 1, 0), hp)
    nxt = lambda hp, bi, rb: (bi, jnp.minimum(rb + 1, nrb - 1), hp)
    blk = (None, ATT_TOK, LANES)
    return pl.pallas_call(
        _attn_kernel,
        grid=(ATT_HEADS // 2, b, nrb),
        in_specs=[
            pl.BlockSpec(blk, cur),
            pl.BlockSpec(blk, prv), pl.BlockSpec(blk, cur), pl.BlockSpec(blk, nxt),
            pl.BlockSpec(blk, prv), pl.BlockSpec(blk, cur), pl.BlockSpec(blk, nxt),
            pl.BlockSpec((2, ATT_VARIANTS, ATT_RB // ATT_SUB, ATT_SUB_TOK, ATT_SUB_KEYS),
                         lambda hp, bi, rb: (hp, 0, 0, 0, 0),
                         pipeline_mode=pl.Buffered(1)),
        ],
        out_specs=pl.BlockSpec(blk, cur),
        out_shape=jax.ShapeDtypeStruct((b, l, ATT_WIDTH), BF16),
        compiler_params=_cparams(("parallel", "parallel", "parallel")),
        name="nbr_attention",
    )(q, k, k, k, v, v, v, bias)


CONV_TL = 512
HALO = 8


def _conv_kernel(xp_ref, xc_ref, xn_ref, w_ref, b_ref, o_ref, buf):
    i = pl.program_id(1)
    nblk = pl.num_programs(1)
    prev = xp_ref[...]
    nxt = xn_ref[...]
    buf[0:HALO] = jnp.where(i == 0, jnp.zeros_like(prev), prev)
    buf[HALO:HALO + CONV_TL] = xc_ref[...]
    buf[HALO + CONV_TL:2 * HALO + CONV_TL] = jnp.where(i == nblk - 1, jnp.zeros_like(nxt), nxt)
    pad = SSD_CONV // 2
    y = jnp.zeros((CONV_TL, CONV_CH), F32) + b_ref[...]
    for j in range(SSD_CONV):
        y = y + buf[HALO - pad + j:HALO - pad + j + CONV_TL, :] * w_ref[j:j + 1, :]
    o_ref[...] = y * (1.0 / (1.0 + jnp.exp(-y)))


def _conv_silu(xbc, conv_w, conv_b):
    b, l, ch = xbc.shape
    tl = CONV_TL
    nblk = l // tl
    per = tl // HALO
    nh = l // HALO
    return pl.pallas_call(
        _conv_kernel,
        grid=(b, nblk),
        in_specs=[
            pl.BlockSpec((None, HALO, ch), lambda bi, i: (bi, jnp.maximum(i * per - 1, 0), 0)),
            pl.BlockSpec((None, tl, ch), lambda bi, i: (bi, i, 0)),
            pl.BlockSpec((None, HALO, ch), lambda bi, i: (bi, jnp.minimum((i + 1) * per, nh - 1), 0)),
            pl.BlockSpec((SSD_CONV, ch), lambda bi, i: (0, 0)),
            pl.BlockSpec((1, ch), lambda bi, i: (0, 0)),
        ],
        out_specs=pl.BlockSpec((None, tl, ch), lambda bi, i: (bi, i, 0)),
        out_shape=jax.ShapeDtypeStruct((b, l, ch), F32),
        scratch_shapes=[pltpu.VMEM((tl + 2 * HALO, ch), F32)],
        compiler_params=_cparams(("parallel", "parallel")),
        name="conv_silu",
    )(xbc, xbc, xbc, conv_w, conv_b)


Q = SSD_CHUNK


def _ssd_direction(xs_ref, bm_ref, cm_ref, dt_ref, bias, a_vec, tri, expand_m, y_ref, s_ref, fwd):
    xx = dt_ref[...] + bias
    dtv = jnp.maximum(xx, 0.0) + jnp.log1p(jnp.exp(-jnp.abs(xx)))
    adt = dtv * a_vec
    hi, mid, lo = _split3(adt)
    cs3 = jnp.dot(tri, jnp.concatenate([hi, mid, lo], axis=1), preferred_element_type=F32)
    cs = cs3[:, 0:LANES] + cs3[:, LANES:2 * LANES] + cs3[:, 2 * LANES:3 * LANES]

    def expand(v):
        h3, m3, l3 = _split3(v)
        r = jnp.dot(jnp.concatenate([h3, m3, l3], axis=0), expand_m, preferred_element_type=F32)
        return r[0:Q] + r[Q:2 * Q] + r[2 * Q:3 * Q]

    dt_x = expand(dtv)
    cs_x = expand(cs)
    last = Q - 1 if fwd else 0
    tot_x = cs_x[last:last + 1, :]
    xdt = xs_ref[...] * dt_x
    xdt_b = xdt.astype(BF16)
    xw = (xdt * jnp.exp(tot_x - cs_x)).astype(BF16)
    scale_off = jnp.exp(cs_x)
    chunk_decay = jnp.exp(tot_x)
    cs_t = cs.T
    row = lax.broadcasted_iota(jnp.int32, (Q, Q), 0)
    col = lax.broadcasted_iota(jnp.int32, (Q, Q), 1)
    tri_mask = (col <= row) if fwd else (col >= row)
    gw = SSD_WIDTH // SSD_GROUPS
    for g in range(SSD_GROUPS):
        bg = bm_ref[:, g * SSD_STATE:(g + 1) * SSD_STATE]
        cg = cm_ref[:, g * SSD_STATE:(g + 1) * SSD_STATE].astype(BF16)
        cb = lax.dot_general(cg, bg.astype(BF16), NT_DIMS, preferred_element_type=F32)
        state = s_ref[g]
        yoff = jnp.dot(cg, state.astype(BF16), preferred_element_type=F32) * scale_off[:, g * gw:(g + 1) * gw]
        bg_t = bg.T.astype(BF16)
        s_ref[g] = state * chunk_decay[:, g * gw:(g + 1) * gw] + jnp.dot(
            bg_t, xw[:, g * gw:(g + 1) * gw], preferred_element_type=F32)
        for pp in range(2):
            pair = 2 * g + pp
            xpair = xdt_b[:, pair * LANES:(pair + 1) * LANES]
            ypair = yoff[:, pp * LANES:(pp + 1) * LANES]
            for hh in range(2):
                hc = 2 * pair + hh + (0 if fwd else SSD_HEADS)
                diff = cs[:, hc:hc + 1] - cs_t[hc:hc + 1, :]
                lm = jnp.where(tri_mask, jnp.exp(jnp.minimum(diff, 0.0)), 0.0)
                mm = (cb * lm).astype(BF16)
                inh = (col >= hh * SSD_HEAD_DIM) & (col < (hh + 1) * SSD_HEAD_DIM)
                ypair = ypair + jnp.dot(mm, jnp.where(inh, xpair, jnp.zeros_like(xpair)),
                                        preferred_element_type=F32)
            y_ref[:, pair * LANES:(pair + 1) * LANES] = ypair


def _ssd_kernel(xs_f, bm_f, cm_f, dt_f, xs_b, bm_b, cm_b, dt_b, bias_ref, alog_ref, tri_ref, e_ref,
                yf_ref, yb_ref, sf_ref, sb_ref):
    @pl.when(pl.program_id(1) == 0)
    def _():
        sf_ref[...] = jnp.zeros_like(sf_ref)
        sb_ref[...] = jnp.zeros_like(sb_ref)

    lane = lax.broadcasted_iota(jnp.int32, (1, LANES), 1)
    bias = bias_ref[...]
    a_vec = jnp.where(lane < 2 * SSD_HEADS, -jnp.exp(alog_ref[...]), 0.0)
    _ssd_direction(xs_f, bm_f, cm_f, dt_f, bias, a_vec, tri_ref[0], e_ref[0], yf_ref, sf_ref, True)
    _ssd_direction(xs_b, bm_b, cm_b, dt_b, bias, a_vec, tri_ref[1], e_ref[1], yb_ref, sb_ref, False)


def _ssd_scan(xbc_act, dt, bias_vec, alog_vec, tri, expand_m):
    b, l, _ = xbc_act.shape
    nc = l // Q
    fw = lambda cb: (lambda bi, c: (bi, c, cb))
    bw = lambda cb: (lambda bi, c: (bi, nc - 1 - c, cb))
    const2 = lambda bi, c: (0, 0)
    const3 = lambda bi, c: (0, 0, 0)

    def chunk_specs(mk):
        return [
            pl.BlockSpec((None, Q, SSD_WIDTH), mk(0)),
            pl.BlockSpec((None, Q, 2 * SSD_STATE), mk(2)),
            pl.BlockSpec((None, Q, 2 * SSD_STATE), mk(3)),
            pl.BlockSpec((None, Q, LANES), mk(0)),
        ]

    return pl.pallas_call(
        _ssd_kernel,
        grid=(b, nc),
        in_specs=chunk_specs(fw) + chunk_specs(bw) + [
            pl.BlockSpec((1, LANES), const2),
            pl.BlockSpec((1, LANES), const2),
            pl.BlockSpec((2, Q, Q), const3),
            pl.BlockSpec((2, LANES, SSD_WIDTH), const3),
        ],
        out_specs=[
            pl.BlockSpec((None, Q, SSD_WIDTH), fw(0)),
            pl.BlockSpec((None, Q, SSD_WIDTH), bw(0)),
        ],
        out_shape=[jax.ShapeDtypeStruct((b, l, SSD_WIDTH), F32)] * 2,
        scratch_shapes=[pltpu.VMEM((SSD_GROUPS, SSD_STATE, 256), F32),
                        pltpu.VMEM((SSD_GROUPS, SSD_STATE, 256), F32)],
        compiler_params=_cparams(("parallel", "arbitrary")),
        name="ssd_scan",
    )(xbc_act, xbc_act, xbc_act, dt, xbc_act, xbc_act, xbc_act, dt, bias_vec, alog_vec, tri, expand_m)


OUT_TM = 512


def _outproj_kernel(x_ref, att_ref, xs_ref, yf_ref, yb_ref, z_ref, dx_ref, snw_ref, wo_ref, x1_ref):
    z = z_ref[...]
    y = xs_ref[...] * dx_ref[...] + yf_ref[...] + yb_ref[...]
    y = y * (z * (1.0 / (1.0 + jnp.exp(-z))))
    ms = jnp.mean(y * y, axis=-1, keepdims=True)
    ssd = (y * lax.rsqrt(ms + EPS) * snw_ref[...]).astype(BF16)
    mixed = jnp.dot(att_ref[...], wo_ref[0:ATT_WIDTH, :], preferred_element_type=F32)
    mixed = mixed + jnp.dot(ssd, wo_ref[ATT_WIDTH:, :], preferred_element_type=F32)
    x1_ref[...] = x_ref[...] + mixed


def _outproj(x, att, xbc_act, yf, yb, z, d_x, snw, w_out):
    n = x.shape[0]
    tm = OUT_TM
    row = lambda i: (i, 0)
    full = lambda i: (0, 0)
    return pl.pallas_call(
        _outproj_kernel,
        grid=(n // tm,),
        in_specs=[
            pl.BlockSpec((tm, D_MODEL), row),
            pl.BlockSpec((tm, ATT_WIDTH), row),
            pl.BlockSpec((tm, SSD_WIDTH), row),
            pl.BlockSpec((tm, SSD_WIDTH), row),
            pl.BlockSpec((tm, SSD_WIDTH), row),
            pl.BlockSpec((tm, SSD_WIDTH), row),
            pl.BlockSpec((1, SSD_WIDTH), full),
            pl.BlockSpec((1, SSD_WIDTH), full),
            pl.BlockSpec((D_MODEL, D_MODEL), full),
        ],
        out_specs=pl.BlockSpec((tm, D_MODEL), row),
        out_shape=jax.ShapeDtypeStruct((n, D_MODEL), F32),
        compiler_params=_cparams(("parallel",)),
        name="outproj",
    )(x, att, xbc_act, yf, yb, z, d_x, snw, w_out)


PEER_TB = 1024
PEER_PIECE = 512
PEER_EC = 1024
PEER_M = PEER_EC // PEER_N_KEYS
BF16_ROWS = 16
N_JT = PEER_N_KEYS // BF16_ROWS


SUBLANES = 8
N_ROWV = PEER_N_KEYS // SUBLANES


def _oddeven_merge_sort_pairs(n):
    out, p = [], 1
    while p < n:
        k = p
        while k >= 1:
            for j in range(k % p, n - k, 2 * k):
                for i in range(min(k, n - j - k)):
                    if (i + j) // (2 * p) == (i + j + k) // (2 * p):
                        out.append((i + j, i + j + k))
            k //= 2
        p *= 2
    return out


SORT16_PAIRS = _oddeven_merge_sort_pairs(PEER_TOPK)


def _compare_exchange(rows, i, j):
    a, b = rows[i], rows[j]
    rows[i] = jnp.maximum(a, b)
    rows[j] = jnp.minimum(a, b)


def _sort_desc(rows):
    for i, j in SORT16_PAIRS:
        if j < len(rows):
            _compare_exchange(rows, i, j)


def _top16_over_sublanes(rows):
    rows = list(rows) + [None] * (PEER_TOPK - len(rows))
    for shift in (4, 2, 1):
        other = [None if r is None else pltpu.roll(r, shift, 0) for r in rows]
        merged = []
        for k in range(PEER_TOPK):
            a, b = rows[k], other[PEER_TOPK - 1 - k]
            merged.append(b if a is None else a if b is None else jnp.maximum(a, b))
        rows = merged
        for stride in (8, 4, 2, 1):
            for k in range(PEER_TOPK):
                if k & stride == 0:
                    _compare_exchange(rows, k, k + stride)
    return rows


def _sum_over_sublanes(x):
    for shift in (4, 2, 1):
        x = x + pltpu.roll(x, shift, 0)
    return x


def _peer_select_tile(s0_s, s1_s, lanes, h, r1_s, e1_s, n_s, c_s):
    rowsl = lambda ref, k: ref[k * SUBLANES:(k + 1) * SUBLANES, lanes]
    s0 = [rowsl(s0_s, k) for k in range(N_ROWV)]
    s1 = [rowsl(s1_s, k) for k in range(N_ROWV)]
    v0, v1 = list(s0), list(s1)
    _sort_desc(v0)
    _sort_desc(v1)
    v0 = _top16_over_sublanes(v0)
    v1 = _top16_over_sublanes(v1)
    sub = lax.broadcasted_iota(jnp.int32, (SUBLANES, LANES), 0)

    def pack(vals):
        out = vals[0]
        for s in range(1, SUBLANES):
            out = jnp.where(sub == s, vals[s], out)
        return out

    v1_lo, v1_hi, v0_hi = pack(v1[0:8]), pack(v1[8:16]), pack(v0[8:16])
    cand = [v0[0] + v1_lo, v0[0] + v1_hi] + [v0[a] + v1_lo for a in range(1, 8)] + [v0_hi + v1[0]]
    csort = list(cand)
    _sort_desc(csort)
    csort = _top16_over_sublanes(csort)
    tau, cmax = csort[PEER_TOPK - 1], csort[0]
    zsum = jnp.zeros_like(tau)
    for cv in cand:
        zsum = zsum + jnp.where(cv >= tau, jnp.exp(cv - cmax), 0.0)
    rz = 1.0 / _sum_over_sublanes(zsum)
    n_a = []
    for a in range(PEER_TOPK):
        cnt = jnp.where(v0[a] + v1_lo >= tau, 1.0, 0.0) + jnp.where(v0[a] + v1_hi >= tau, 1.0, 0.0)
        n_a.append(_sum_over_sublanes(cnt))
    for m in range(N_JT):
        packed_r, packed_e = [], []
        for k in (2 * m, 2 * m + 1):
            n = jnp.zeros_like(tau)
            r = jnp.full_like(tau, float(PEER_TOPK))
            for a in range(PEER_TOPK):
                n = jnp.where(s0[k] == v0[a], n_a[a], n)
                r = jnp.where(s1[k] == v1[a], float(a), r)
            n_s[h, k, :, lanes] = n
            c_s[h, k, :, lanes] = jnp.exp(s0[k] - v0[0]) * rz
            packed_r.append(r)
            packed_e.append(jnp.exp(s1[k] - v1[0]))
        r1_s[h, m, :, lanes] = jnp.concatenate(packed_r, axis=0).astype(BF16)
        e1_s[h, m, :, lanes] = jnp.concatenate(packed_e, axis=0).astype(BF16)


def _peer_kernel(x1_ref, fw_ref, wqt_ref, sk_ref, u_ref, vt_ref, nw_ref, o_ref,
                 hf_s, qt_s, r1_s, e1_s, n_s, c_s, s0_s, s1_s, acc_s):
    c = pl.program_id(1)

    @pl.when(c == 0)
    def _prepare():
        x1 = x1_ref[...]
        ms = jnp.mean(x1 * x1, axis=-1, keepdims=True)
        hf_s[...] = (x1 * lax.rsqrt(ms + EPS) * fw_ref[...]).T.astype(BF16)
        qt_s[...] = jnp.dot(wqt_ref[...], hf_s[...], preferred_element_type=F32).astype(BF16)
        acc_s[...] = jnp.zeros_like(acc_s)

        def per_head(h, carry):
            r0 = pl.multiple_of(h * 2 * PEER_D_KEY, 2 * PEER_D_KEY)
            s0 = jnp.dot(sk_ref[2 * h], qt_s[pl.ds(r0, PEER_D_KEY), :], preferred_element_type=F32)
            s1 = jnp.dot(sk_ref[2 * h + 1], qt_s[pl.ds(r0 + PEER_D_KEY, PEER_D_KEY), :],
                         preferred_element_type=F32)
            s0_s[...] = s0
            s1_s[...] = s1

            def per_tile(lt, carry2):
                lanes = pl.ds(pl.multiple_of(lt * LANES, LANES), LANES)
                _peer_select_tile(s0_s, s1_s, lanes, h, r1_s, e1_s, n_s, c_s)
                return carry2

            lax.fori_loop(0, PEER_TB // LANES, per_tile, 0)
            return carry

        lax.fori_loop(0, PEER_HEADS, per_head, 0)

    n_pieces = PEER_TB // PEER_PIECE

    def expert_scores(p):
        return jnp.dot(u_ref[...], hf_s[:, p * PEER_PIECE:(p + 1) * PEER_PIECE],
                       preferred_element_type=F32)

    xu_next = expert_scores(0)
    for p in range(n_pieces):
        lanes = slice(p * PEER_PIECE, (p + 1) * PEER_PIECE)
        xu = xu_next
        if p + 1 < n_pieces:
            xu_next = expert_scores(p + 1)
        gates = []
        for ii in range(PEER_M):
            g = jnp.zeros((N_JT, BF16_ROWS, PEER_PIECE), BF16)
            for h in range(PEER_HEADS):
                nb = jnp.broadcast_to(n_s[h, c, ii:ii + 1, lanes], (BF16_ROWS, PEER_PIECE)).astype(BF16)
                cb = jnp.broadcast_to(c_s[h, c, ii:ii + 1, lanes], (BF16_ROWS, PEER_PIECE)).astype(BF16)
                sel = jnp.where(r1_s[h, :, :, lanes] < nb[None], e1_s[h, :, :, lanes], jnp.zeros((), BF16))
                g = g + sel * cb[None]
            gates.append(g.reshape(PEER_N_KEYS, PEER_PIECE))
        xb = xu.astype(BF16)
        act = (0.5 * xb) * (1.0 + lax.erf(xb * (2.0 ** -0.5)))
        a_t = act * jnp.concatenate(gates, axis=0)
        acc_s[:, lanes] += jnp.dot(vt_ref[...], a_t, preferred_element_type=F32)

    @pl.when(c == pl.num_programs(1) - 1)
    def _finish():
        y = x1_ref[...] + acc_s[...].T
        ms = jnp.mean(y * y, axis=-1, keepdims=True)
        o_ref[...] = y * lax.rsqrt(ms + EPS) * nw_ref[...]


def _peer(x1, fw, wq_t, sub_keys, exp_u, exp_vt, nw):
    n = x1.shape[0]
    tb, ec = PEER_TB, PEER_EC
    n_chunks = PEER_N_EXPERTS // ec
    tok = lambda t, c: (t, 0)
    const2 = lambda t, c: (0, 0)
    head_tiles = (PEER_HEADS, N_JT, BF16_ROWS, tb)
    key_tiles = (PEER_HEADS, N_ROWV, SUBLANES, tb)
    assert PEER_M == SUBLANES
    return pl.pallas_call(
        _peer_kernel,
        grid=(n // tb, n_chunks),
        in_specs=[
            pl.BlockSpec((tb, D_MODEL), tok, pipeline_mode=pl.Buffered(1)),
            pl.BlockSpec((1, D_MODEL), const2),
            pl.BlockSpec((2 * PEER_HEADS * PEER_D_KEY, D_MODEL), const2, pipeline_mode=pl.Buffered(1)),
            pl.BlockSpec((2 * PEER_HEADS, PEER_N_KEYS, PEER_D_KEY), lambda t, c: (0, 0, 0)),
            pl.BlockSpec((ec, D_MODEL), lambda t, c: (c, 0)),
            pl.BlockSpec((D_MODEL, ec), lambda t, c: (0, c)),
            pl.BlockSpec((1, D_MODEL), const2),
        ],
        out_specs=pl.BlockSpec((tb, D_MODEL), tok),
        out_shape=jax.ShapeDtypeStruct((n, D_MODEL), F32),
        scratch_shapes=[
            pltpu.VMEM((D_MODEL, tb), BF16),
            pltpu.VMEM((2 * PEER_HEADS * PEER_D_KEY, tb), BF16),
            pltpu.VMEM(head_tiles, BF16),
            pltpu.VMEM(head_tiles, BF16),
            pltpu.VMEM(key_tiles, F32),
            pltpu.VMEM(key_tiles, F32),
            pltpu.VMEM((PEER_N_KEYS, tb), F32),
            pltpu.VMEM((PEER_N_KEYS, tb), F32),
            pltpu.VMEM((D_MODEL, tb), F32),
        ],
        compiler_params=_cparams(("parallel", "arbitrary")),
        name="peer",
    )(x1, fw, wq_t, sub_keys, exp_u, exp_vt, nw)


def _lane_vec(two_by_heads):
    v = two_by_heads.astype(F32).reshape(1, 2 * SSD_HEADS)
    return jnp.pad(v, ((0, 0), (0, LANES - 2 * SSD_HEADS)))


def _trunk(x, p):
    b, l, d = x.shape
    n = b * l
    xf = x.reshape(n, d)
    q, k, v, z, xbc, dt = _inproj(xf, p["norm_mix_w"], p["w_main"], p["w_dt"])
    att = _attention(q.reshape(b, l, -1), k.reshape(b, l, -1), v.reshape(b, l, -1), p["att_bias"])
    xbc_act = _conv_silu(xbc.reshape(b, l, -1), p["conv_w"], p["conv_b"])
    yf, yb = _ssd_scan(xbc_act, dt.reshape(b, l, -1), p["dt_bias_vec"], p["a_log_vec"], p["tri"], p["expand"])
    x1 = _outproj(xf, att.reshape(n, -1), xbc_act.reshape(n, -1), yf.reshape(n, -1), yb.reshape(n, -1),
                  z, p["d_x"], p["ssd_norm_w"], p["w_out"])
    y = _peer(x1, p["norm_ffn_w"], p["wq_t"], p["sub_keys"], p["exp_u"], p["exp_vt"], p["norm_final_w"])
    return y.reshape(b, l, d)


def kernel(x_prompt, x_sample, norm_mix_w, w_in, rpb, conv_w, conv_b, dt_bias, a_log, d_skip, ssd_norm_w,
           w_out, norm_ffn_w, w_query, sub_keys, expert_u, expert_v, norm_final_w):
    assert w_in.shape[0] == 1, "single layer"
    w_in0 = w_in[0]
    w_dt = w_in0[:, MAIN_COLS:]
    w_dt = jnp.concatenate([w_dt, w_dt, jnp.zeros((D_MODEL, LANES - 2 * SSD_HEADS), w_dt.dtype)], axis=1)
    r = jnp.arange(Q)
    tril = (r[None, :] <= r[:, None])
    tri = jnp.stack([tril, tril.T]).astype(BF16)
    hid = jnp.arange(SSD_WIDTH) // SSD_HEAD_DIM
    lane = jnp.arange(LANES)
    expand = jnp.stack([lane[:, None] == hid[None, :],
                        lane[:, None] == hid[None, :] + SSD_HEADS]).astype(BF16)
    p = {
        "norm_mix_w": norm_mix_w[0].reshape(1, -1),
        "w_main": w_in0[:, :MAIN_COLS].astype(BF16),
        "w_dt": w_dt.astype(BF16),
        "att_bias": _attn_bias(rpb[0]),
        "conv_w": conv_w[0],
        "conv_b": conv_b[0].reshape(1, -1),
        "dt_bias_vec": _lane_vec(dt_bias[0]),
        "a_log_vec": _lane_vec(a_log[0]),
        "tri": tri,
        "expand": expand,
        "d_x": jnp.repeat(d_skip[0].astype(F32), SSD_HEAD_DIM).reshape(1, -1),
        "ssd_norm_w": ssd_norm_w[0].reshape(1, -1),
        "w_out": w_out[0].astype(BF16),
        "norm_ffn_w": norm_ffn_w[0].reshape(1, -1),
        "wq_t": w_query[0].T.astype(BF16),
        "sub_keys": sub_keys[0].reshape(2 * PEER_HEADS, PEER_N_KEYS, PEER_D_KEY).astype(BF16),
        "exp_u": expert_u[0].astype(BF16),
        "exp_vt": expert_v[0].T.astype(BF16),
        "norm_final_w": norm_final_w.reshape(1, -1),
    }
    return (_trunk(x_prompt, p), _trunk(x_sample, p))
```

```python
import functools
import math

import jax
import jax.numpy as jnp
from jax import lax
from jax.experimental import pallas as pl
from jax.experimental.pallas import tpu as pltpu

F32 = jnp.float32
BF16 = jnp.bfloat16

D_MODEL = 1024
GRID_W = 64
ATT_HEADS = 8
ATT_HEAD_DIM = 64
ATT_WIDTH = 512
WIN_H = 8
WIN_W = 16
SSD_HEADS = 8
SSD_HEAD_DIM = 64
SSD_WIDTH = 512
SSD_GROUPS = 2
SSD_STATE = 128
SSD_CONV = 5
SSD_CHUNK = 128
CONV_CH = 1024
MAIN_COLS = 3 * ATT_WIDTH + SSD_WIDTH + CONV_CH
PEER_HEADS = 8
PEER_N_KEYS = 128
PEER_N_EXPERTS = PEER_N_KEYS * PEER_N_KEYS
PEER_D_KEY = 128
PEER_TOPK = 16
EPS = 1e-6

LANES = 128
VMEM_LIMIT = 60 * 1024 * 1024
NEG_BIG = -1e30
NT_DIMS = (((1,), (1,)), ((), ()))


def _cparams(sem):
    return pltpu.CompilerParams(dimension_semantics=sem, vmem_limit_bytes=VMEM_LIMIT)


def _split3(v):
    hi = v.astype(BF16)
    r1 = v - hi.astype(F32)
    mid = r1.astype(BF16)
    lo = (r1 - mid.astype(F32)).astype(BF16)
    return hi, mid, lo


IN_TM = 512


def _inproj_kernel(x_ref, nw_ref, w_ref, wdt_ref, q_ref, k_ref, v_ref, z_ref, xbc_ref, dt_ref):
    x = x_ref[...]
    ms = jnp.mean(x * x, axis=-1, keepdims=True)
    h = (x * lax.rsqrt(ms + EPS) * nw_ref[...]).astype(BF16)

    def proj(lo, hi):
        return jnp.dot(h, w_ref[:, lo:hi], preferred_element_type=F32)

    q_ref[...] = proj(0, 512).astype(BF16)
    k_ref[...] = proj(512, 1024).astype(BF16)
    v_ref[...] = proj(1024, 1536).astype(BF16)
    z_ref[...] = proj(1536, 2048)
    xbc_ref[...] = proj(2048, 3072)
    dt_ref[...] = jnp.dot(h, wdt_ref[...], preferred_element_type=F32)


def _inproj(x, nw, w_main, w_dt):
    n = x.shape[0]
    tm = IN_TM
    row = lambda i: (i, 0)
    full = lambda i: (0, 0)
    return pl.pallas_call(
        _inproj_kernel,
        grid=(n // tm,),
        in_specs=[
            pl.BlockSpec((tm, D_MODEL), row),
            pl.BlockSpec((1, D_MODEL), full),
            pl.BlockSpec((D_MODEL, MAIN_COLS), full),
            pl.BlockSpec((D_MODEL, LANES), full),
        ],
        out_specs=[
            pl.BlockSpec((tm, 512), row),
            pl.BlockSpec((tm, 512), row),
            pl.BlockSpec((tm, 512), row),
            pl.BlockSpec((tm, 512), row),
            pl.BlockSpec((tm, 1024), row),
            pl.BlockSpec((tm, LANES), row),
        ],
        out_shape=[
            jax.ShapeDtypeStruct((n, 512), BF16),
            jax.ShapeDtypeStruct((n, 512), BF16),
            jax.ShapeDtypeStruct((n, 512), BF16),
            jax.ShapeDtypeStruct((n, 512), F32),
            jax.ShapeDtypeStruct((n, 1024), F32),
            jax.ShapeDtypeStruct((n, LANES), F32),
        ],
        compiler_params=_cparams(("parallel",)),
        name="inproj",
    )(x, nw, w_main, w_dt)


ATT_RB = 8
ATT_TOK = ATT_RB * GRID_W
ATT_HALO = WIN_H // 2
ATT_UROWS = ATT_RB + 2 * ATT_HALO
ATT_KEYS = ATT_UROWS * GRID_W
ATT_VARIANTS = 3
ATT_SUB = 4
ATT_SUB_TOK = ATT_SUB * GRID_W
ATT_SUB_KEYS = (ATT_SUB + 2 * ATT_HALO) * GRID_W


def _attn_kernel(q_ref, kp_ref, kc_ref, kn_ref, vp_ref, vc_ref, vn_ref, bias_ref, o_ref):
    rb = pl.program_id(2)
    nrb = pl.num_programs(2)
    variant = jnp.where(rb == 0, 0, jnp.where(rb == nrb - 1, 2, 1))
    halo = ATT_HALO * GRID_W
    kw = jnp.concatenate([kp_ref[ATT_TOK - halo:, :], kc_ref[...], kn_ref[0:halo, :]], axis=0)
    vw = jnp.concatenate([vp_ref[ATT_TOK - halo:, :], vc_ref[...], vn_ref[0:halo, :]], axis=0)
    lane = lax.broadcasted_iota(jnp.int32, (ATT_SUB_TOK, LANES), 1)
    for sb in range(ATT_RB // ATT_SUB):
        q = q_ref[sb * ATT_SUB_TOK:(sb + 1) * ATT_SUB_TOK, :]
        ks = kw[sb * ATT_SUB_TOK:sb * ATT_SUB_TOK + ATT_SUB_KEYS, :]
        vs = vw[sb * ATT_SUB_TOK:sb * ATT_SUB_TOK + ATT_SUB_KEYS, :]
        acc = jnp.zeros((ATT_SUB_TOK, LANES), F32)
        for hh in range(2):
            inh = (lane >= hh * ATT_HEAD_DIM) & (lane < (hh + 1) * ATT_HEAD_DIM)
            qm = jnp.where(inh, q, jnp.zeros_like(q))
            s = lax.dot_general(qm, ks, NT_DIMS, preferred_element_type=F32)
            s = s * (ATT_HEAD_DIM ** -0.5) + bias_ref[hh, variant, sb]
            m = jnp.max(s, axis=-1, keepdims=True)
            p = jnp.exp(s - m)
            l = jnp.sum(p, axis=-1, keepdims=True)
            o = jnp.dot(p.astype(BF16), vs, preferred_element_type=F32) / l
            acc = jnp.where(inh, o, acc)
        o_ref[sb * ATT_SUB_TOK:(sb + 1) * ATT_SUB_TOK, :] = acc.astype(BF16)


def _attn_bias(rpb):
    c = jnp.arange(GRID_W)
    col_start = jnp.clip(c - WIN_W // 2, 0, GRID_W - WIN_W)
    col_in = (c[None, :] >= col_start[:, None]) & (c[None, :] < col_start[:, None] + WIN_W)
    dc = jnp.clip(c[None, :] - c[:, None], -(WIN_W - 1), WIN_W - 1) + (WIN_W - 1)
    t = rpb.astype(F32)[:, :, dc]
    t = jnp.where(col_in[None, None], t, NEG_BIG)
    n_sub = ATT_RB // ATT_SUB
    sub_rows = ATT_SUB + 2 * ATT_HALO
    qr = (jnp.arange(n_sub)[:, None, None] * ATT_SUB + jnp.arange(ATT_SUB)[None, :, None])
    u = jnp.arange(n_sub)[:, None, None] * ATT_SUB + jnp.arange(sub_rows)[None, None, :]
    dr = jnp.clip(u - ATT_HALO - qr + (WIN_H - 1), 0, 2 * WIN_H - 2)
    first = jnp.maximum(qr - ATT_HALO, 0) + ATT_HALO
    last = jnp.minimum(qr - ATT_HALO, 0) + ATT_HALO
    starts = jnp.stack([jnp.broadcast_to(s, dr.shape) for s in (first, qr, last)])
    valid = (u[None] >= starts) & (u[None] < starts + WIN_H)
    t = t[:, dr]
    t = jnp.where(valid[None, :, :, :, :, None, None], t[:, None], NEG_BIG)
    t = jnp.transpose(t, (0, 1, 2, 3, 5, 4, 6))
    return t.reshape(ATT_HEADS, ATT_VARIANTS, n_sub, ATT_SUB_TOK, ATT_SUB_KEYS)


def _attention(q, k, v, bias):
    b, l, _ = q.shape
    rows = l // GRID_W
    assert rows % ATT_RB == 0 and rows // ATT_RB >= 2
    nrb = rows // ATT_RB
    cur = lambda hp, bi, rb: (bi, rb, hp)
    prv = lambda hp, bi, rb: (bi, jnp.maximum(rb - 1, 0), hp)
    nxt = lambda hp, bi, rb: (bi, jnp.minimum(rb + 1, nrb - 1), hp)
    blk = (None, ATT_TOK, LANES)
    return pl.pallas_call(
        _attn_kernel,
        grid=(ATT_HEADS // 2, b, nrb),
        in_specs=[
            pl.BlockSpec(blk, cur),
            pl.BlockSpec(blk, prv), pl.BlockSpec(blk, cur), pl.BlockSpec(blk, nxt),
            pl.BlockSpec(blk, prv), pl.BlockSpec(blk, cur), pl.BlockSpec(blk, nxt),
            pl.BlockSpec((2, ATT_VARIANTS, ATT_RB // ATT_SUB, ATT_SUB_TOK, ATT_SUB_KEYS),
                         lambda hp, bi, rb: (hp, 0, 0, 0, 0),
                         pipeline_mode=pl.Buffered(1)),
        ],
        out_specs=pl.BlockSpec(blk, cur),
        out_shape=jax.ShapeDtypeStruct((b, l, ATT_WIDTH), BF16),
        compiler_params=_cparams(("parallel", "parallel", "parallel")),
        name="nbr_attention",
    )(q, k, k, k, v, v, v, bias)


CONV_TL = 512
HALO = 8


def _conv_kernel(xp_ref, xc_ref, xn_ref, w_ref, b_ref, o_ref, buf):
    i = pl.program_id(1)
    nblk = pl.num_programs(1)
    prev = xp_ref[...]
    nxt = xn_ref[...]
    buf[0:HALO] = jnp.where(i == 0, jnp.zeros_like(prev), prev)
    buf[HALO:HALO + CONV_TL] = xc_ref[...]
    buf[HALO + CONV_TL:2 * HALO + CONV_TL] = jnp.where(i == nblk - 1, jnp.zeros_like(nxt), nxt)
    pad = SSD_CONV // 2
    y = jnp.zeros((CONV_TL, CONV_CH), F32) + b_ref[...]
    for j in range(SSD_CONV):
        y = y + buf[HALO - pad + j:HALO - pad + j + CONV_TL, :] * w_ref[j:j + 1, :]
    o_ref[...] = y * (1.0 / (1.0 + jnp.exp(-y)))


def _conv_silu(xbc, conv_w, conv_b):
    b, l, ch = xbc.shape
    tl = CONV_TL
    nblk = l // tl
    per = tl // HALO
    nh = l // HALO
    return pl.pallas_call(
        _conv_kernel,
        grid=(b, nblk),
        in_specs=[
            pl.BlockSpec((None, HALO, ch), lambda bi, i: (bi, jnp.maximum(i * per - 1, 0), 0)),
            pl.BlockSpec((None, tl, ch), lambda bi, i: (bi, i, 0)),
            pl.BlockSpec((None, HALO, ch), lambda bi, i: (bi, jnp.minimum((i + 1) * per, nh - 1), 0)),
            pl.BlockSpec((SSD_CONV, ch), lambda bi, i: (0, 0)),
            pl.BlockSpec((1, ch), lambda bi, i: (0, 0)),
        ],
        out_specs=pl.BlockSpec((None, tl, ch), lambda bi, i: (bi, i, 0)),
        out_shape=jax.ShapeDtypeStruct((b, l, ch), F32),
        scratch_shapes=[pltpu.VMEM((tl + 2 * HALO, ch), F32)],
        compiler_params=_cparams(("parallel", "parallel")),
        name="conv_silu",
    )(xbc, xbc, xbc, conv_w, conv_b)


Q = SSD_CHUNK


def _ssd_direction(xs_ref, bm_ref, cm_ref, dt_ref, bias, a_vec, tri, expand_m, y_ref, s_ref, fwd):
    xx = dt_ref[...] + bias
    dtv = jnp.maximum(xx, 0.0) + jnp.log1p(jnp.exp(-jnp.abs(xx)))
    adt = dtv * a_vec
    hi, mid, lo = _split3(adt)
    cs3 = jnp.dot(tri, jnp.concatenate([hi, mid, lo], axis=1), preferred_element_type=F32)
    cs = cs3[:, 0:LANES] + cs3[:, LANES:2 * LANES] + cs3[:, 2 * LANES:3 * LANES]

    def expand(v):
        h3, m3, l3 = _split3(v)
        r = jnp.dot(jnp.concatenate([h3, m3, l3], axis=0), expand_m, preferred_element_type=F32)
        return r[0:Q] + r[Q:2 * Q] + r[2 * Q:3 * Q]

    dt_x = expand(dtv)
    cs_x = expand(cs)
    last = Q - 1 if fwd else 0
    tot_x = cs_x[last:last + 1, :]
    xdt = xs_ref[...] * dt_x
    xdt_b = xdt.astype(BF16)
    xw = (xdt * jnp.exp(tot_x - cs_x)).astype(BF16)
    scale_off = jnp.exp(cs_x)
    chunk_decay = jnp.exp(tot_x)
    cs_t = cs.T
    row = lax.broadcasted_iota(jnp.int32, (Q, Q), 0)
    col = lax.broadcasted_iota(jnp.int32, (Q, Q), 1)
    tri_mask = (col <= row) if fwd else (col >= row)
    gw = SSD_WIDTH // SSD_GROUPS
    for g in range(SSD_GROUPS):
        bg = bm_ref[:, g * SSD_STATE:(g + 1) * SSD_STATE]
        cg = cm_ref[:, g * SSD_STATE:(g + 1) * SSD_STATE].astype(BF16)
        cb = lax.dot_general(cg, bg.astype(BF16), NT_DIMS, preferred_element_type=F32)
        state = s_ref[g]
        yoff = jnp.dot(cg, state.astype(BF16), preferred_element_type=F32) * scale_off[:, g * gw:(g + 1) * gw]
        bg_t = bg.T.astype(BF16)
        s_ref[g] = state * chunk_decay[:, g * gw:(g + 1) * gw] + jnp.dot(
            bg_t, xw[:, g * gw:(g + 1) * gw], preferred_element_type=F32)
        for pp in range(2):
            pair = 2 * g + pp
            xpair = xdt_b[:, pair * LANES:(pair + 1) * LANES]
            ypair = yoff[:, pp * LANES:(pp + 1) * LANES]
            for hh in range(2):
                hc = 2 * pair + hh + (0 if fwd else SSD_HEADS)
                diff = cs[:, hc:hc + 1] - cs_t[hc:hc + 1, :]
                lm = jnp.where(tri_mask, jnp.exp(jnp.minimum(diff, 0.0)), 0.0)
                mm = (cb * lm).astype(BF16)
                inh = (col >= hh * SSD_HEAD_DIM) & (col < (hh + 1) * SSD_HEAD_DIM)
                ypair = ypair + jnp.dot(mm, jnp.where(inh, xpair, jnp.zeros_like(xpair)),
                                        preferred_element_type=F32)
            y_ref[:, pair * LANES:(pair + 1) * LANES] = ypair


def _ssd_kernel(xs_f, bm_f, cm_f, dt_f, xs_b, bm_b, cm_b, dt_b, bias_ref, alog_ref, tri_ref, e_ref,
                yf_ref, yb_ref, sf_ref, sb_ref):
    @pl.when(pl.program_id(1) == 0)
    def _():
        sf_ref[...] = jnp.zeros_like(sf_ref)
        sb_ref[...] = jnp.zeros_like(sb_ref)

    lane = lax.broadcasted_iota(jnp.int32, (1, LANES), 1)
    bias = bias_ref[...]
    a_vec = jnp.where(lane < 2 * SSD_HEADS, -jnp.exp(alog_ref[...]), 0.0)
    _ssd_direction(xs_f, bm_f, cm_f, dt_f, bias, a_vec, tri_ref[0], e_ref[0], yf_ref, sf_ref, True)
    _ssd_direction(xs_b, bm_b, cm_b, dt_b, bias, a_vec, tri_ref[1], e_ref[1], yb_ref, sb_ref, False)


def _ssd_scan(xbc_act, dt, bias_vec, alog_vec, tri, expand_m):
    b, l, _ = xbc_act.shape
    nc = l // Q
    fw = lambda cb: (lambda bi, c: (bi, c, cb))
    bw = lambda cb: (lambda bi, c: (bi, nc - 1 - c, cb))
    const2 = lambda bi, c: (0, 0)
    const3 = lambda bi, c: (0, 0, 0)

    def chunk_specs(mk):
        return [
            pl.BlockSpec((None, Q, SSD_WIDTH), mk(0)),
            pl.BlockSpec((None, Q, 2 * SSD_STATE), mk(2)),
            pl.BlockSpec((None, Q, 2 * SSD_STATE), mk(3)),
            pl.BlockSpec((None, Q, LANES), mk(0)),
        ]

    return pl.pallas_call(
        _ssd_kernel,
        grid=(b, nc),
        in_specs=chunk_specs(fw) + chunk_specs(bw) + [
            pl.BlockSpec((1, LANES), const2),
            pl.BlockSpec((1, LANES), const2),
            pl.BlockSpec((2, Q, Q), const3),
            pl.BlockSpec((2, LANES, SSD_WIDTH), const3),
        ],
        out_specs=[
            pl.BlockSpec((None, Q, SSD_WIDTH), fw(0)),
            pl.BlockSpec((None, Q, SSD_WIDTH), bw(0)),
        ],
        out_shape=[jax.ShapeDtypeStruct((b, l, SSD_WIDTH), F32)] * 2,
        scratch_shapes=[pltpu.VMEM((SSD_GROUPS, SSD_STATE, 256), F32),
                        pltpu.VMEM((SSD_GROUPS, SSD_STATE, 256), F32)],
        compiler_params=_cparams(("parallel", "arbitrary")),
        name="ssd_scan",
    )(xbc_act, xbc_act, xbc_act, dt, xbc_act, xbc_act, xbc_act, dt, bias_vec, alog_vec, tri, expand_m)


OUT_TM = 512


def _outproj_kernel(x_ref, att_ref, xs_ref, yf_ref, yb_ref, z_ref, dx_ref, snw_ref, wo_ref, x1_ref):
    z = z_ref[...]
    y = xs_ref[...] * dx_ref[...] + yf_ref[...] + yb_ref[...]
    y = y * (z * (1.0 / (1.0 + jnp.exp(-z))))
    ms = jnp.mean(y * y, axis=-1, keepdims=True)
    ssd = (y * lax.rsqrt(ms + EPS) * snw_ref[...]).astype(BF16)
    mixed = jnp.dot(att_ref[...], wo_ref[0:ATT_WIDTH, :], preferred_element_type=F32)
    mixed = mixed + jnp.dot(ssd, wo_ref[ATT_WIDTH:, :], preferred_element_type=F32)
    x1_ref[...] = x_ref[...] + mixed


def _outproj(x, att, xbc_act, yf, yb, z, d_x, snw, w_out):
    n = x.shape[0]
    tm = OUT_TM
    row = lambda i: (i, 0)
    full = lambda i: (0, 0)
    return pl.pallas_call(
        _outproj_kernel,
        grid=(n // tm,),
        in_specs=[
            pl.BlockSpec((tm, D_MODEL), row),
            pl.BlockSpec((tm, ATT_WIDTH), row),
            pl.BlockSpec((tm, SSD_WIDTH), row),
            pl.BlockSpec((tm, SSD_WIDTH), row),
            pl.BlockSpec((tm, SSD_WIDTH), row),
            pl.BlockSpec((tm, SSD_WIDTH), row),
            pl.BlockSpec((1, SSD_WIDTH), full),
            pl.BlockSpec((1, SSD_WIDTH), full),
            pl.BlockSpec((D_MODEL, D_MODEL), full),
        ],
        out_specs=pl.BlockSpec((tm, D_MODEL), row),
        out_shape=jax.ShapeDtypeStruct((n, D_MODEL), F32),
        compiler_params=_cparams(("parallel",)),
        name="outproj",
    )(x, att, xbc_act, yf, yb, z, d_x, snw, w_out)


PEER_TB = 1024
PEER_PIECE = 1024
PEER_EC = 1024
PEER_M = PEER_EC // PEER_N_KEYS
BF16_ROWS = 16
N_JT = PEER_N_KEYS // BF16_ROWS


SUBLANES = 8
N_ROWV = PEER_N_KEYS // SUBLANES


def _oddeven_merge_sort_pairs(n):
    out, p = [], 1
    while p < n:
        k = p
        while k >= 1:
            for j in range(k % p, n - k, 2 * k):
                for i in range(min(k, n - j - k)):
                    if (i + j) // (2 * p) == (i + j + k) // (2 * p):
                        out.append((i + j, i + j + k))
            k //= 2
        p *= 2
    return out


SORT16_PAIRS = _oddeven_merge_sort_pairs(PEER_TOPK)


def _compare_exchange(rows, i, j):
    a, b = rows[i], rows[j]
    rows[i] = jnp.maximum(a, b)
    rows[j] = jnp.minimum(a, b)


def _sort_desc(rows):
    for i, j in SORT16_PAIRS:
        if j < len(rows):
            _compare_exchange(rows, i, j)


def _top16_over_sublanes(rows):
    rows = list(rows) + [None] * (PEER_TOPK - len(rows))
    for shift in (4, 2, 1):
        other = [None if r is None else pltpu.roll(r, shift, 0) for r in rows]
        merged = []
        for k in range(PEER_TOPK):
            a, b = rows[k], other[PEER_TOPK - 1 - k]
            merged.append(b if a is None else a if b is None else jnp.maximum(a, b))
        rows = merged
        for stride in (8, 4, 2, 1):
            for k in range(PEER_TOPK):
                if k & stride == 0:
                    _compare_exchange(rows, k, k + stride)
    return rows


def _sum_over_sublanes(x):
    for shift in (4, 2, 1):
        x = x + pltpu.roll(x, shift, 0)
    return x


def _peer_select_tile(s0_s, s1_s, lanes, h, r1_s, e1_s, n_s, c_s):
    rowsl = lambda ref, k: ref[k * SUBLANES:(k + 1) * SUBLANES, lanes]
    s0 = [rowsl(s0_s, k) for k in range(N_ROWV)]
    s1 = [rowsl(s1_s, k) for k in range(N_ROWV)]
    v0, v1 = list(s0), list(s1)
    _sort_desc(v0)
    _sort_desc(v1)
    v0 = _top16_over_sublanes(v0)
    v1 = _top16_over_sublanes(v1)
    sub = lax.broadcasted_iota(jnp.int32, (SUBLANES, LANES), 0)

    def pack(vals):
        out = vals[0]
        for s in range(1, SUBLANES):
            out = jnp.where(sub == s, vals[s], out)
        return out

    v1_lo, v1_hi, v0_hi = pack(v1[0:8]), pack(v1[8:16]), pack(v0[8:16])
    cand = [v0[0] + v1_lo, v0[0] + v1_hi] + [v0[a] + v1_lo for a in range(1, 8)] + [v0_hi + v1[0]]
    csort = list(cand)
    _sort_desc(csort)
    csort = _top16_over_sublanes(csort)
    tau, cmax = csort[PEER_TOPK - 1], csort[0]
    zsum = jnp.zeros_like(tau)
    for cv in cand:
        zsum = zsum + jnp.where(cv >= tau, jnp.exp(cv - cmax), 0.0)
    rz = 1.0 / _sum_over_sublanes(zsum)
    hit = lambda x: jnp.where(x >= tau, 1.0, 0.0)
    n_a = []
    for a in range(PEER_TOPK):
        reach = PEER_TOPK // (a + 1)
        if reach > SUBLANES:
            n_a.append(_sum_over_sublanes(hit(v0[a] + v1_lo) + hit(v0[a] + v1_hi)))
        elif reach > 3:
            n_a.append(_sum_over_sublanes(hit(v0[a] + v1_lo)))
        else:
            n_a.append(sum(hit(v0[a] + v1[b]) for b in range(reach)))
    for m in range(N_JT):
        packed_r, packed_e = [], []
        for k in (2 * m, 2 * m + 1):
            n = jnp.zeros_like(tau)
            r = jnp.full_like(tau, float(PEER_TOPK))
            for a in range(PEER_TOPK):
                n = jnp.where(s0[k] == v0[a], n_a[a], n)
                r = jnp.where(s1[k] == v1[a], float(a), r)
            n_s[h, k, :, lanes] = n
            c_s[h, k, :, lanes] = jnp.exp(s0[k] - v0[0]) * rz
            packed_r.append(r)
            packed_e.append(jnp.exp(s1[k] - v1[0]))
        r1_s[h, m, :, lanes] = jnp.concatenate(packed_r, axis=0).astype(BF16)
        e1_s[h, m, :, lanes] = jnp.concatenate(packed_e, axis=0).astype(BF16)


def _peer_kernel(x1_ref, fw_ref, wqt_ref, sk_ref, u_ref, vt_ref, nw_ref, o_ref,
                 hf_s, qt_s, r1_s, e1_s, n_s, c_s, s0_s, s1_s, acc_s):
    c = pl.program_id(1)

    @pl.when(c == 0)
    def _prepare():
        x1 = x1_ref[...]
        ms = jnp.mean(x1 * x1, axis=-1, keepdims=True)
        hf_s[...] = (x1 * lax.rsqrt(ms + EPS) * fw_ref[...]).T.astype(BF16)
        qt_s[...] = jnp.dot(wqt_ref[...], hf_s[...], preferred_element_type=F32).astype(BF16)
        acc_s[...] = jnp.zeros_like(acc_s)

        def per_head(h, carry):
            r0 = pl.multiple_of(h * 2 * PEER_D_KEY, 2 * PEER_D_KEY)
            s0 = jnp.dot(sk_ref[2 * h], qt_s[pl.ds(r0, PEER_D_KEY), :], preferred_element_type=F32)
            s1 = jnp.dot(sk_ref[2 * h + 1], qt_s[pl.ds(r0 + PEER_D_KEY, PEER_D_KEY), :],
                         preferred_element_type=F32)
            s0_s[...] = s0
            s1_s[...] = s1

            def per_tile(lt, carry2):
                lanes = pl.ds(pl.multiple_of(lt * LANES, LANES), LANES)
                _peer_select_tile(s0_s, s1_s, lanes, h, r1_s, e1_s, n_s, c_s)
                return carry2

            lax.fori_loop(0, PEER_TB // LANES, per_tile, 0)
            return carry

        lax.fori_loop(0, PEER_HEADS, per_head, 0)

    n_pieces = PEER_TB // PEER_PIECE

    def expert_scores(p):
        return jnp.dot(u_ref[...], hf_s[:, p * PEER_PIECE:(p + 1) * PEER_PIECE],
                       preferred_element_type=F32)

    xu_next = expert_scores(0)
    for p in range(n_pieces):
        lanes = slice(p * PEER_PIECE, (p + 1) * PEER_PIECE)
        xu = xu_next
        if p + 1 < n_pieces:
            xu_next = expert_scores(p + 1)
        gates = []
        for ii in range(PEER_M):
            g = jnp.zeros((N_JT, BF16_ROWS, PEER_PIECE), BF16)
            for h in range(PEER_HEADS):
                nb = jnp.broadcast_to(n_s[h, c, ii:ii + 1, lanes], (BF16_ROWS, PEER_PIECE)).astype(BF16)
                cb = jnp.broadcast_to(c_s[h, c, ii:ii + 1, lanes], (BF16_ROWS, PEER_PIECE)).astype(BF16)
                sel = jnp.where(r1_s[h, :, :, lanes] < nb[None], e1_s[h, :, :, lanes], jnp.zeros((), BF16))
                g = g + sel * cb[None]
            gates.append(g.reshape(PEER_N_KEYS, PEER_PIECE))
        xb = xu.astype(BF16)
        act = (0.5 * xb) * (1.0 + lax.erf(xb * (2.0 ** -0.5)))
        a_t = act * jnp.concatenate(gates, axis=0)
        acc_s[:, lanes] += jnp.dot(vt_ref[...], a_t, preferred_element_type=F32)

    @pl.when(c == pl.num_programs(1) - 1)
    def _finish():
        y = x1_ref[...] + acc_s[...].T
        ms = jnp.mean(y * y, axis=-1, keepdims=True)
        o_ref[...] = y * lax.rsqrt(ms + EPS) * nw_ref[...]


def _peer(x1, fw, wq_t, sub_keys, exp_u, exp_vt, nw):
    n = x1.shape[0]
    tb, ec = PEER_TB, PEER_EC
    n_chunks = PEER_N_EXPERTS // ec
    tok = lambda t, c: (t, 0)
    const2 = lambda t, c: (0, 0)
    head_tiles = (PEER_HEADS, N_JT, BF16_ROWS, tb)
    key_tiles = (PEER_HEADS, N_ROWV, SUBLANES, tb)
    assert PEER_M == SUBLANES
    return pl.pallas_call(
        _peer_kernel,
        grid=(n // tb, n_chunks),
        in_specs=[
            pl.BlockSpec((tb, D_MODEL), tok, pipeline_mode=pl.Buffered(1)),
            pl.BlockSpec((1, D_MODEL), const2),
            pl.BlockSpec((2 * PEER_HEADS * PEER_D_KEY, D_MODEL), const2, pipeline_mode=pl.Buffered(1)),
            pl.BlockSpec((2 * PEER_HEADS, PEER_N_KEYS, PEER_D_KEY), lambda t, c: (0, 0, 0)),
            pl.BlockSpec((ec, D_MODEL), lambda t, c: (c, 0)),
            pl.BlockSpec((D_MODEL, ec), lambda t, c: (0, c)),
            pl.BlockSpec((1, D_MODEL), const2),
        ],
        out_specs=pl.BlockSpec((tb, D_MODEL), tok),
        out_shape=jax.ShapeDtypeStruct((n, D_MODEL), F32),
        scratch_shapes=[
            pltpu.VMEM((D_MODEL, tb), BF16),
            pltpu.VMEM((2 * PEER_HEADS * PEER_D_KEY, tb), BF16),
            pltpu.VMEM(head_tiles, BF16),
            pltpu.VMEM(head_tiles, BF16),
            pltpu.VMEM(key_tiles, F32),
            pltpu.VMEM(key_tiles, F32),
            pltpu.VMEM((PEER_N_KEYS, tb), F32),
            pltpu.VMEM((PEER_N_KEYS, tb), F32),
            pltpu.VMEM((D_MODEL, tb), F32),
        ],
        compiler_params=_cparams(("parallel", "arbitrary")),
        name="peer",
    )(x1, fw, wq_t, sub_keys, exp_u, exp_vt, nw)


def _lane_vec(two_by_heads):
    v = two_by_heads.astype(F32).reshape(1, 2 * SSD_HEADS)
    return jnp.pad(v, ((0, 0), (0, LANES - 2 * SSD_HEADS)))


def _trunk(x, p):
    b, l, d = x.shape
    n = b * l
    xf = x.reshape(n, d)
    q, k, v, z, xbc, dt = _inproj(xf, p["norm_mix_w"], p["w_main"], p["w_dt"])
    att = _attention(q.reshape(b, l, -1), k.reshape(b, l, -1), v.reshape(b, l, -1), p["att_bias"])
    xbc_act = _conv_silu(xbc.reshape(b, l, -1), p["conv_w"], p["conv_b"])
    yf, yb = _ssd_scan(xbc_act, dt.reshape(b, l, -1), p["dt_bias_vec"], p["a_log_vec"], p["tri"], p["expand"])
    x1 = _outproj(xf, att.reshape(n, -1), xbc_act.reshape(n, -1), yf.reshape(n, -1), yb.reshape(n, -1),
                  z, p["d_x"], p["ssd_norm_w"], p["w_out"])
    y = _peer(x1, p["norm_ffn_w"], p["wq_t"], p["sub_keys"], p["exp_u"], p["exp_vt"], p["norm_final_w"])
    return y.reshape(b, l, d)


def kernel(x_prompt, x_sample, norm_mix_w, w_in, rpb, conv_w, conv_b, dt_bias, a_log, d_skip, ssd_norm_w,
           w_out, norm_ffn_w, w_query, sub_keys, expert_u, expert_v, norm_final_w):
    assert w_in.shape[0] == 1, "single layer"
    w_in0 = w_in[0]
    w_dt = w_in0[:, MAIN_COLS:]
    w_dt = jnp.concatenate([w_dt, w_dt, jnp.zeros((D_MODEL, LANES - 2 * SSD_HEADS), w_dt.dtype)], axis=1)
    r = jnp.arange(Q)
    tril = (r[None, :] <= r[:, None])
    tri = jnp.stack([tril, tril.T]).astype(BF16)
    hid = jnp.arange(SSD_WIDTH) // SSD_HEAD_DIM
    lane = jnp.arange(LANES)
    expand = jnp.stack([lane[:, None] == hid[None, :],
                        lane[:, None] == hid[None, :] + SSD_HEADS]).astype(BF16)
    p = {
        "norm_mix_w": norm_mix_w[0].reshape(1, -1),
        "w_main": w_in0[:, :MAIN_COLS].astype(BF16),
        "w_dt": w_dt.astype(BF16),
        "att_bias": _attn_bias(rpb[0]),
        "conv_w": conv_w[0],
        "conv_b": conv_b[0].reshape(1, -1),
        "dt_bias_vec": _lane_vec(dt_bias[0]),
        "a_log_vec": _lane_vec(a_log[0]),
        "tri": tri,
        "expand": expand,
        "d_x": jnp.repeat(d_skip[0].astype(F32), SSD_HEAD_DIM).reshape(1, -1),
        "ssd_norm_w": ssd_norm_w[0].reshape(1, -1),
        "w_out": w_out[0].astype(BF16),
        "norm_ffn_w": norm_ffn_w[0].reshape(1, -1),
        "wq_t": w_query[0].T.astype(BF16),
        "sub_keys": sub_keys[0].reshape(2 * PEER_HEADS, PEER_N_KEYS, PEER_D_KEY).astype(BF16),
        "exp_u": expert_u[0].astype(BF16),
        "exp_vt": expert_v[0].T.astype(BF16),
        "norm_final_w": norm_final_w.reshape(1, -1),
    }
    return (_trunk(x_prompt, p), _trunk(x_sample, p))
```

```python
import functools
import math

import jax
import jax.numpy as jnp
from jax import lax
from jax.experimental import pallas as pl
from jax.experimental.pallas import tpu as pltpu

F32 = jnp.float32
BF16 = jnp.bfloat16

D_MODEL = 1024
GRID_W = 64
ATT_HEADS = 8
ATT_HEAD_DIM = 64
ATT_WIDTH = 512
WIN_H = 8
WIN_W = 16
SSD_HEADS = 8
SSD_HEAD_DIM = 64
SSD_WIDTH = 512
SSD_GROUPS = 2
SSD_STATE = 128
SSD_CONV = 5
SSD_CHUNK = 128
CONV_CH = 1024
MAIN_COLS = 3 * ATT_WIDTH + SSD_WIDTH + CONV_CH
PEER_HEADS = 8
PEER_N_KEYS = 128
PEER_N_EXPERTS = PEER_N_KEYS * PEER_N_KEYS
PEER_D_KEY = 128
PEER_TOPK = 16
EPS = 1e-6

LANES = 128
VMEM_LIMIT = 60 * 1024 * 1024
NEG_BIG = -1e30
NT_DIMS = (((1,), (1,)), ((), ()))


def _cparams(sem):
    return pltpu.CompilerParams(dimension_semantics=sem, vmem_limit_bytes=VMEM_LIMIT)


def _split3(v):
    hi = v.astype(BF16)
    r1 = v - hi.astype(F32)
    mid = r1.astype(BF16)
    lo = (r1 - mid.astype(F32)).astype(BF16)
    return hi, mid, lo


IN_TM = 512


def _inproj_kernel(x_ref, nw_ref, w_ref, wdt_ref, q_ref, k_ref, v_ref, z_ref, xbc_ref, dt_ref):
    x = x_ref[...]
    ms = jnp.mean(x * x, axis=-1, keepdims=True)
    h = (x * lax.rsqrt(ms + EPS) * nw_ref[...]).astype(BF16)

    def proj(lo, hi):
        return jnp.dot(h, w_ref[:, lo:hi], preferred_element_type=F32)

    q_ref[...] = proj(0, 512).astype(BF16)
    k_ref[...] = proj(512, 1024).astype(BF16)
    v_ref[...] = proj(1024, 1536).astype(BF16)
    z_ref[...] = proj(1536, 2048).astype(BF16)
    xbc_ref[...] = proj(2048, 3072)
    dt_ref[...] = jnp.dot(h, wdt_ref[...], preferred_element_type=F32)


def _inproj(x, nw, w_main, w_dt):
    n = x.shape[0]
    tm = IN_TM
    row = lambda i: (i, 0)
    full = lambda i: (0, 0)
    return pl.pallas_call(
        _inproj_kernel,
        grid=(n // tm,),
        in_specs=[
            pl.BlockSpec((tm, D_MODEL), row),
            pl.BlockSpec((1, D_MODEL), full),
            pl.BlockSpec((D_MODEL, MAIN_COLS), full),
            pl.BlockSpec((D_MODEL, LANES), full),
        ],
        out_specs=[
            pl.BlockSpec((tm, 512), row),
            pl.BlockSpec((tm, 512), row),
            pl.BlockSpec((tm, 512), row),
            pl.BlockSpec((tm, 512), row),
            pl.BlockSpec((tm, 1024), row),
            pl.BlockSpec((tm, LANES), row),
        ],
        out_shape=[
            jax.ShapeDtypeStruct((n, 512), BF16),
            jax.ShapeDtypeStruct((n, 512), BF16),
            jax.ShapeDtypeStruct((n, 512), BF16),
            jax.ShapeDtypeStruct((n, 512), BF16),
            jax.ShapeDtypeStruct((n, 1024), F32),
            jax.ShapeDtypeStruct((n, LANES), F32),
        ],
        compiler_params=_cparams(("parallel",)),
        name="inproj",
    )(x, nw, w_main, w_dt)


ATT_RB = 8
ATT_TOK = ATT_RB * GRID_W
ATT_HALO = WIN_H // 2
ATT_UROWS = ATT_RB + 2 * ATT_HALO
ATT_KEYS = ATT_UROWS * GRID_W
ATT_VARIANTS = 3
ATT_SUB = 4
ATT_PAIRS = 2
ATT_SUB_TOK = ATT_SUB * GRID_W
ATT_SUB_KEYS = (ATT_SUB + 2 * ATT_HALO) * GRID_W


def _attn_kernel(q_ref, kp_ref, kc_ref, kn_ref, vp_ref, vc_ref, vn_ref, bias_ref, o_ref):
    rb = pl.program_id(2)
    nrb = pl.num_programs(2)
    variant = jnp.where(rb == 0, 0, jnp.where(rb == nrb - 1, 2, 1))
    halo = ATT_HALO * GRID_W
    kw = jnp.concatenate([kp_ref[ATT_TOK - halo:, :], kc_ref[...], kn_ref[0:halo, :]], axis=0)
    vw = jnp.concatenate([vp_ref[ATT_TOK - halo:, :], vc_ref[...], vn_ref[0:halo, :]], axis=0)
    lane = lax.broadcasted_iota(jnp.int32, (ATT_SUB_TOK, LANES), 1)
    scale = jnp.asarray(ATT_HEAD_DIM ** -0.5, BF16)
    for pr in range(ATT_PAIRS):
        cols = slice(pr * LANES, (pr + 1) * LANES)
        for sb in range(ATT_RB // ATT_SUB):
            q = q_ref[sb * ATT_SUB_TOK:(sb + 1) * ATT_SUB_TOK, cols] * scale
            ks = kw[sb * ATT_SUB_TOK:sb * ATT_SUB_TOK + ATT_SUB_KEYS, cols]
            vs = vw[sb * ATT_SUB_TOK:sb * ATT_SUB_TOK + ATT_SUB_KEYS, cols]
            acc = jnp.zeros((ATT_SUB_TOK, LANES), F32)
            for hh in range(2):
                inh = (lane >= hh * ATT_HEAD_DIM) & (lane < (hh + 1) * ATT_HEAD_DIM)
                qm = jnp.where(inh, q, jnp.zeros_like(q))
                s = lax.dot_general(qm, ks, NT_DIMS, preferred_element_type=F32)
                s = s + bias_ref[2 * pr + hh, variant, sb]
                m = jnp.max(s, axis=-1, keepdims=True)
                p = jnp.exp(s - m)
                l = jnp.sum(p, axis=-1, keepdims=True)
                o = jnp.dot(p.astype(BF16), vs, preferred_element_type=F32) / l
                acc = jnp.where(inh, o, acc)
            o_ref[sb * ATT_SUB_TOK:(sb + 1) * ATT_SUB_TOK, cols] = acc.astype(BF16)


def _attn_bias(rpb):
    c = jnp.arange(GRID_W)
    col_start = jnp.clip(c - WIN_W // 2, 0, GRID_W - WIN_W)
    col_in = (c[None, :] >= col_start[:, None]) & (c[None, :] < col_start[:, None] + WIN_W)
    dc = jnp.clip(c[None, :] - c[:, None], -(WIN_W - 1), WIN_W - 1) + (WIN_W - 1)
    t = rpb.astype(F32)[:, :, dc]
    t = jnp.where(col_in[None, None], t, NEG_BIG)
    n_sub = ATT_RB // ATT_SUB
    sub_rows = ATT_SUB + 2 * ATT_HALO
    qr = (jnp.arange(n_sub)[:, None, None] * ATT_SUB + jnp.arange(ATT_SUB)[None, :, None])
    u = jnp.arange(n_sub)[:, None, None] * ATT_SUB + jnp.arange(sub_rows)[None, None, :]
    dr = jnp.clip(u - ATT_HALO - qr + (WIN_H - 1), 0, 2 * WIN_H - 2)
    first = jnp.maximum(qr - ATT_HALO, 0) + ATT_HALO
    last = jnp.minimum(qr - ATT_HALO, 0) + ATT_HALO
    starts = jnp.stack([jnp.broadcast_to(s, dr.shape) for s in (first, qr, last)])
    valid = (u[None] >= starts) & (u[None] < starts + WIN_H)
    t = t[:, dr]
    t = jnp.where(valid[None, :, :, :, :, None, None], t[:, None], NEG_BIG)
    t = jnp.transpose(t, (0, 1, 2, 3, 5, 4, 6))
    return t.reshape(ATT_HEADS, ATT_VARIANTS, n_sub, ATT_SUB_TOK, ATT_SUB_KEYS)


def _attention(q, k, v, bias):
    b, l, _ = q.shape
    rows = l // GRID_W
    assert rows % ATT_RB == 0 and rows // ATT_RB >= 2
    nrb = rows // ATT_RB
    cur = lambda hp, bi, rb: (bi, rb, hp)
    prv = lambda hp, bi, rb: (bi, jnp.maximum(rb - 1, 0), hp)
    nxt = lambda hp, bi, rb: (bi, jnp.minimum(rb + 1, nrb - 1), hp)
    blk = (None, ATT_TOK, ATT_PAIRS * LANES)
    return pl.pallas_call(
        _attn_kernel,
        grid=(ATT_HEADS // (2 * ATT_PAIRS), b, nrb),
        in_specs=[
            pl.BlockSpec(blk, cur),
            pl.BlockSpec(blk, prv), pl.BlockSpec(blk, cur), pl.BlockSpec(blk, nxt),
            pl.BlockSpec(blk, prv), pl.BlockSpec(blk, cur), pl.BlockSpec(blk, nxt),
            pl.BlockSpec((2 * ATT_PAIRS, ATT_VARIANTS, ATT_RB // ATT_SUB, ATT_SUB_TOK, ATT_SUB_KEYS),
                         lambda hp, bi, rb: (hp, 0, 0, 0, 0),
                         pipeline_mode=pl.Buffered(1)),
        ],
        out_specs=pl.BlockSpec(blk, cur),
        out_shape=jax.ShapeDtypeStruct((b, l, ATT_WIDTH), BF16),
        compiler_params=_cparams(("parallel", "parallel", "parallel")),
        name="nbr_attention",
    )(q, k, k, k, v, v, v, bias)


CONV_TL = 512
HALO = 8


def _conv_kernel(xp_ref, xc_ref, xn_ref, w_ref, b_ref, o_ref, buf):
    i = pl.program_id(1)
    nblk = pl.num_programs(1)
    prev = xp_ref[...]
    nxt = xn_ref[...]
    buf[0:HALO] = jnp.where(i == 0, jnp.zeros_like(prev), prev)
    buf[HALO:HALO + CONV_TL] = xc_ref[...]
    buf[HALO + CONV_TL:2 * HALO + CONV_TL] = jnp.where(i == nblk - 1, jnp.zeros_like(nxt), nxt)
    pad = SSD_CONV // 2
    y = jnp.zeros((CONV_TL, CONV_CH), F32) + b_ref[...]
    for j in range(SSD_CONV):
        y = y + buf[HALO - pad + j:HALO - pad + j + CONV_TL, :] * w_ref[j:j + 1, :]
    o_ref[...] = y * (1.0 / (1.0 + jnp.exp(-y)))


def _conv_silu(xbc, conv_w, conv_b):
    b, l, ch = xbc.shape
    tl = CONV_TL
    nblk = l // tl
    per = tl // HALO
    nh = l // HALO
    return pl.pallas_call(
        _conv_kernel,
        grid=(b, nblk),
        in_specs=[
            pl.BlockSpec((None, HALO, ch), lambda bi, i: (bi, jnp.maximum(i * per - 1, 0), 0)),
            pl.BlockSpec((None, tl, ch), lambda bi, i: (bi, i, 0)),
            pl.BlockSpec((None, HALO, ch), lambda bi, i: (bi, jnp.minimum((i + 1) * per, nh - 1), 0)),
            pl.BlockSpec((SSD_CONV, ch), lambda bi, i: (0, 0)),
            pl.BlockSpec((1, ch), lambda bi, i: (0, 0)),
        ],
        out_specs=pl.BlockSpec((None, tl, ch), lambda bi, i: (bi, i, 0)),
        out_shape=jax.ShapeDtypeStruct((b, l, ch), F32),
        scratch_shapes=[pltpu.VMEM((tl + 2 * HALO, ch), F32)],
        compiler_params=_cparams(("parallel", "parallel")),
        name="conv_silu",
    )(xbc, xbc, xbc, conv_w, conv_b)


Q = SSD_CHUNK
SSD_CPS = 2


def _ssd_direction(xs_ref, bm_ref, cm_ref, dt_ref, bias, a_vec, tri, expand_m, y_ref, s_ref, fwd, rows):
    xx = dt_ref[rows, :] + bias
    dtv = jnp.maximum(xx, 0.0) + jnp.log1p(jnp.exp(-jnp.abs(xx)))
    adt = dtv * a_vec
    hi, mid, lo = _split3(adt)
    cs3 = jnp.dot(tri, jnp.concatenate([hi, mid, lo], axis=1), preferred_element_type=F32)
    cs = cs3[:, 0:LANES] + cs3[:, LANES:2 * LANES] + cs3[:, 2 * LANES:3 * LANES]

    def expand(v):
        h3, m3, _ = _split3(v)
        r = jnp.dot(jnp.concatenate([h3, m3], axis=0), expand_m, preferred_element_type=F32)
        return r[0:Q] + r[Q:2 * Q]

    dt_x = expand(dtv)
    cs_x = expand(cs)
    last = Q - 1 if fwd else 0
    tot_x = cs_x[last:last + 1, :]
    xdt = xs_ref[rows, :] * dt_x
    xdt_b = xdt.astype(BF16)
    xw = (xdt * jnp.exp(tot_x - cs_x)).astype(BF16)
    scale_off = jnp.exp(cs_x)
    chunk_decay = jnp.exp(tot_x)
    cs_t = cs.T
    row = lax.broadcasted_iota(jnp.int32, (Q, Q), 0)
    col = lax.broadcasted_iota(jnp.int32, (Q, Q), 1)
    tri_mask = (col <= row) if fwd else (col >= row)
    gw = SSD_WIDTH // SSD_GROUPS
    for g in range(SSD_GROUPS):
        bg = bm_ref[rows, g * SSD_STATE:(g + 1) * SSD_STATE]
        cg = cm_ref[rows, g * SSD_STATE:(g + 1) * SSD_STATE].astype(BF16)
        cb = lax.dot_general(cg, bg.astype(BF16), NT_DIMS, preferred_element_type=F32)
        state = s_ref[g]
        yoff = jnp.dot(cg, state.astype(BF16), preferred_element_type=F32) * scale_off[:, g * gw:(g + 1) * gw]
        bg_t = bg.T.astype(BF16)
        s_ref[g] = state * chunk_decay[:, g * gw:(g + 1) * gw] + jnp.dot(
            bg_t, xw[:, g * gw:(g + 1) * gw], preferred_element_type=F32)
        for pp in range(2):
            pair = 2 * g + pp
            xpair = xdt_b[:, pair * LANES:(pair + 1) * LANES]
            ypair = yoff[:, pp * LANES:(pp + 1) * LANES]
            for hh in range(2):
                hc = 2 * pair + hh + (0 if fwd else SSD_HEADS)
                diff = cs[:, hc:hc + 1] - cs_t[hc:hc + 1, :]
                lm = jnp.where(tri_mask, jnp.exp(jnp.minimum(diff, 0.0)), 0.0)
                mm = (cb * lm).astype(BF16)
                inh = (col >= hh * SSD_HEAD_DIM) & (col < (hh + 1) * SSD_HEAD_DIM)
                ypair = ypair + jnp.dot(mm, jnp.where(inh, xpair, jnp.zeros_like(xpair)),
                                        preferred_element_type=F32)
            y_ref[rows, pair * LANES:(pair + 1) * LANES] = ypair.astype(y_ref.dtype)


def _ssd_kernel(xs_f, bm_f, cm_f, dt_f, xs_b, bm_b, cm_b, dt_b, bias_ref, alog_ref, tri_ref, e_ref,
                yf_ref, yb_ref, sf_ref, sb_ref):
    @pl.when(pl.program_id(1) == 0)
    def _():
        sf_ref[...] = jnp.zeros_like(sf_ref)
        sb_ref[...] = jnp.zeros_like(sb_ref)

    lane = lax.broadcasted_iota(jnp.int32, (1, LANES), 1)
    bias = bias_ref[...]
    a_vec = jnp.where(lane < 2 * SSD_HEADS, -jnp.exp(alog_ref[...]), 0.0)
    for k in range(SSD_CPS):
        up = slice(k * Q, (k + 1) * Q)
        down = slice((SSD_CPS - 1 - k) * Q, (SSD_CPS - k) * Q)
        _ssd_direction(xs_f, bm_f, cm_f, dt_f, bias, a_vec, tri_ref[0], e_ref[0], yf_ref, sf_ref, True, up)
        _ssd_direction(xs_b, bm_b, cm_b, dt_b, bias, a_vec, tri_ref[1], e_ref[1], yb_ref, sb_ref, False, down)


def _ssd_scan(xbc_act, dt, bias_vec, alog_vec, tri, expand_m):
    b, l, _ = xbc_act.shape
    nc = l // (Q * SSD_CPS)
    blk = Q * SSD_CPS
    fw = lambda cb: (lambda bi, c: (bi, c, cb))
    bw = lambda cb: (lambda bi, c: (bi, nc - 1 - c, cb))
    const2 = lambda bi, c: (0, 0)
    const3 = lambda bi, c: (0, 0, 0)

    def chunk_specs(mk):
        return [
            pl.BlockSpec((None, blk, SSD_WIDTH), mk(0)),
            pl.BlockSpec((None, blk, 2 * SSD_STATE), mk(2)),
            pl.BlockSpec((None, blk, 2 * SSD_STATE), mk(3)),
            pl.BlockSpec((None, blk, LANES), mk(0)),
        ]

    return pl.pallas_call(
        _ssd_kernel,
        grid=(b, nc),
        in_specs=chunk_specs(fw) + chunk_specs(bw) + [
            pl.BlockSpec((1, LANES), const2),
            pl.BlockSpec((1, LANES), const2),
            pl.BlockSpec((2, Q, Q), const3),
            pl.BlockSpec((2, LANES, SSD_WIDTH), const3),
        ],
        out_specs=[
            pl.BlockSpec((None, blk, SSD_WIDTH), fw(0)),
            pl.BlockSpec((None, blk, SSD_WIDTH), bw(0)),
        ],
        out_shape=[jax.ShapeDtypeStruct((b, l, SSD_WIDTH), BF16)] * 2,
        scratch_shapes=[pltpu.VMEM((SSD_GROUPS, SSD_STATE, 256), F32),
                        pltpu.VMEM((SSD_GROUPS, SSD_STATE, 256), F32)],
        compiler_params=_cparams(("parallel", "arbitrary")),
        name="ssd_scan",
    )(xbc_act, xbc_act, xbc_act, dt, xbc_act, xbc_act, xbc_act, dt, bias_vec, alog_vec, tri, expand_m)


OUT_TM = 512


def _outproj_kernel(x_ref, att_ref, xs_ref, yf_ref, yb_ref, z_ref, dx_ref, snw_ref, wo_ref, x1_ref):
    z = z_ref[...].astype(F32)
    y = xs_ref[...] * dx_ref[...] + yf_ref[...].astype(F32) + yb_ref[...].astype(F32)
    y = y * (z * (1.0 / (1.0 + jnp.exp(-z))))
    ms = jnp.mean(y * y, axis=-1, keepdims=True)
    ssd = (y * lax.rsqrt(ms + EPS) * snw_ref[...]).astype(BF16)
    mixed = jnp.dot(att_ref[...], wo_ref[0:ATT_WIDTH, :], preferred_element_type=F32)
    mixed = mixed + jnp.dot(ssd, wo_ref[ATT_WIDTH:, :], preferred_element_type=F32)
    x1_ref[...] = x_ref[...] + mixed


def _outproj(x, att, xbc_act, yf, yb, z, d_x, snw, w_out):
    n = x.shape[0]
    tm = OUT_TM
    row = lambda i: (i, 0)
    full = lambda i: (0, 0)
    return pl.pallas_call(
        _outproj_kernel,
        grid=(n // tm,),
        in_specs=[
            pl.BlockSpec((tm, D_MODEL), row),
            pl.BlockSpec((tm, ATT_WIDTH), row),
            pl.BlockSpec((tm, SSD_WIDTH), row),
            pl.BlockSpec((tm, SSD_WIDTH), row),
            pl.BlockSpec((tm, SSD_WIDTH), row),
            pl.BlockSpec((tm, SSD_WIDTH), row),
            pl.BlockSpec((1, SSD_WIDTH), full),
            pl.BlockSpec((1, SSD_WIDTH), full),
            pl.BlockSpec((D_MODEL, D_MODEL), full),
        ],
        out_specs=pl.BlockSpec((tm, D_MODEL), row),
        out_shape=jax.ShapeDtypeStruct((n, D_MODEL), F32),
        compiler_params=_cparams(("parallel",)),
        name="outproj",
    )(x, att, xbc_act, yf, yb, z, d_x, snw, w_out)


PEER_TB = 1024
PEER_PIECE = 1024
PEER_EC = 1024
PEER_M = PEER_EC // PEER_N_KEYS
BF16_ROWS = 16
N_JT = PEER_N_KEYS // BF16_ROWS


SUBLANES = 8
N_ROWV = PEER_N_KEYS // SUBLANES


def _oddeven_merge_sort_pairs(n):
    out, p = [], 1
    while p < n:
        k = p
        while k >= 1:
            for j in range(k % p, n - k, 2 * k):
                for i in range(min(k, n - j - k)):
                    if (i + j) // (2 * p) == (i + j + k) // (2 * p):
                        out.append((i + j, i + j + k))
            k //= 2
        p *= 2
    return out


SORT16_PAIRS = _oddeven_merge_sort_pairs(PEER_TOPK)


def _compare_exchange(rows, i, j):
    a, b = rows[i], rows[j]
    rows[i] = jnp.maximum(a, b)
    rows[j] = jnp.minimum(a, b)


def _sort_desc(rows):
    for i, j in SORT16_PAIRS:
        if j < len(rows):
            _compare_exchange(rows, i, j)


def _top16_over_sublanes(rows):
    rows = list(rows) + [None] * (PEER_TOPK - len(rows))
    for shift in (4, 2, 1):
        other = [None if r is None else pltpu.roll(r, shift, 0) for r in rows]
        merged = []
        for k in range(PEER_TOPK):
            a, b = rows[k], other[PEER_TOPK - 1 - k]
            merged.append(b if a is None else a if b is None else jnp.maximum(a, b))
        rows = merged
        for stride in (8, 4, 2, 1):
            for k in range(PEER_TOPK):
                if k & stride == 0:
                    _compare_exchange(rows, k, k + stride)
    return rows


def _sum_over_sublanes(x):
    for shift in (4, 2, 1):
        x = x + pltpu.roll(x, shift, 0)
    return x


def _peer_select_tile(s0_s, s1_s, lanes, h, r1_s, e1_s, n_s, c_s):
    rowsl = lambda ref, k: ref[k * SUBLANES:(k + 1) * SUBLANES, lanes]
    s0 = [rowsl(s0_s, k) for k in range(N_ROWV)]
    s1 = [rowsl(s1_s, k) for k in range(N_ROWV)]
    v0, v1 = list(s0), list(s1)
    _sort_desc(v0)
    _sort_desc(v1)
    v0 = _top16_over_sublanes(v0)
    v1 = _top16_over_sublanes(v1)
    sub = lax.broadcasted_iota(jnp.int32, (SUBLANES, LANES), 0)

    def pack(vals):
        out = vals[0]
        for s in range(1, SUBLANES):
            out = jnp.where(sub == s, vals[s], out)
        return out

    v1_lo, v1_hi, v0_hi = pack(v1[0:8]), pack(v1[8:16]), pack(v0[8:16])
    cand = [v0[0] + v1_lo, v0[0] + v1_hi] + [v0[a] + v1_lo for a in range(1, 8)] + [v0_hi + v1[0]]
    csort = list(cand)
    _sort_desc(csort)
    csort = _top16_over_sublanes(csort)
    tau, cmax = csort[PEER_TOPK - 1], csort[0]
    zsum = jnp.zeros_like(tau)
    for cv in cand:
        zsum = zsum + jnp.where(cv >= tau, jnp.exp(cv - cmax), 0.0)
    rz = 1.0 / _sum_over_sublanes(zsum)
    hit = lambda x: jnp.where(x >= tau, 1.0, 0.0)
    n_a = []
    for a in range(PEER_TOPK):
        reach = PEER_TOPK // (a + 1)
        if reach > SUBLANES:
            n_a.append(_sum_over_sublanes(hit(v0[a] + v1_lo) + hit(v0[a] + v1_hi)))
        elif reach > 3:
            n_a.append(_sum_over_sublanes(hit(v0[a] + v1_lo)))
        else:
            n_a.append(sum(hit(v0[a] + v1[b]) for b in range(reach)))
    for m in range(N_JT):
        packed_r, packed_e = [], []
        for k in (2 * m, 2 * m + 1):
            n = jnp.zeros_like(tau)
            r = jnp.full_like(tau, float(PEER_TOPK))
            for a in range(PEER_TOPK):
                n = jnp.where(s0[k] == v0[a], n_a[a], n)
                r = jnp.where(s1[k] == v1[a], float(a), r)
            n_s[h, k, :, lanes] = n
            c_s[h, k, :, lanes] = jnp.exp(s0[k] - v0[0]) * rz
            packed_r.append(r)
            packed_e.append(jnp.exp(s1[k] - v1[0]))
        r1_s[h, m, :, lanes] = jnp.concatenate(packed_r, axis=0).astype(BF16)
        e1_s[h, m, :, lanes] = jnp.concatenate(packed_e, axis=0).astype(BF16)


def _peer_kernel(x1_ref, fw_ref, wqt_ref, sk_ref, u_ref, vt_ref, nw_ref, o_ref,
                 hf_s, qt_s, r1_s, e1_s, n_s, c_s, s0_s, s1_s, acc_s):
    c = pl.program_id(1)

    @pl.when(c == 0)
    def _prepare():
        x1 = x1_ref[...]
        ms = jnp.mean(x1 * x1, axis=-1, keepdims=True)
        hf_s[...] = (x1 * lax.rsqrt(ms + EPS) * fw_ref[...]).T.astype(BF16)
        qt_s[...] = jnp.dot(wqt_ref[...], hf_s[...], preferred_element_type=F32).astype(BF16)
        acc_s[...] = jnp.zeros_like(acc_s)

        def per_head(h, carry):
            r0 = pl.multiple_of(h * 2 * PEER_D_KEY, 2 * PEER_D_KEY)
            s0 = jnp.dot(sk_ref[2 * h], qt_s[pl.ds(r0, PEER_D_KEY), :], preferred_element_type=F32)
            s1 = jnp.dot(sk_ref[2 * h + 1], qt_s[pl.ds(r0 + PEER_D_KEY, PEER_D_KEY), :],
                         preferred_element_type=F32)
            s0_s[...] = s0
            s1_s[...] = s1

            def per_tile(lt, carry2):
                lanes = pl.ds(pl.multiple_of(lt * LANES, LANES), LANES)
                _peer_select_tile(s0_s, s1_s, lanes, h, r1_s, e1_s, n_s, c_s)
                return carry2

            lax.fori_loop(0, PEER_TB // LANES, per_tile, 0)
            return carry

        lax.fori_loop(0, PEER_HEADS, per_head, 0)

    n_pieces = PEER_TB // PEER_PIECE

    def expert_scores(p):
        return jnp.dot(u_ref[...], hf_s[:, p * PEER_PIECE:(p + 1) * PEER_PIECE],
                       preferred_element_type=F32)

    xu_next = expert_scores(0)
    for p in range(n_pieces):
        lanes = slice(p * PEER_PIECE, (p + 1) * PEER_PIECE)
        xu = xu_next
        if p + 1 < n_pieces:
            xu_next = expert_scores(p + 1)
        gates = []
        for ii in range(PEER_M):
            g = jnp.zeros((N_JT, BF16_ROWS, PEER_PIECE), BF16)
            for h in range(PEER_HEADS):
                nb = jnp.broadcast_to(n_s[h, c, ii:ii + 1, lanes], (BF16_ROWS, PEER_PIECE)).astype(BF16)
                cb = jnp.broadcast_to(c_s[h, c, ii:ii + 1, lanes], (BF16_ROWS, PEER_PIECE)).astype(BF16)
                sel = jnp.where(r1_s[h, :, :, lanes] < nb[None], e1_s[h, :, :, lanes], jnp.zeros((), BF16))
                g = g + sel * cb[None]
            gates.append(g.reshape(PEER_N_KEYS, PEER_PIECE))
        xb = xu.astype(BF16)
        act = (0.5 * xb) * (1.0 + lax.erf(xb * (2.0 ** -0.5)))
        a_t = act * jnp.concatenate(gates, axis=0)
        acc_s[:, lanes] += jnp.dot(vt_ref[...], a_t, preferred_element_type=F32)

    @pl.when(c == pl.num_programs(1) - 1)
    def _finish():
        y = x1_ref[...] + acc_s[...].T
        ms = jnp.mean(y * y, axis=-1, keepdims=True)
        o_ref[...] = y * lax.rsqrt(ms + EPS) * nw_ref[...]


def _peer(x1, fw, wq_t, sub_keys, exp_u, exp_vt, nw):
    n = x1.shape[0]
    tb, ec = PEER_TB, PEER_EC
    n_chunks = PEER_N_EXPERTS // ec
    tok = lambda t, c: (t, 0)
    const2 = lambda t, c: (0, 0)
    head_tiles = (PEER_HEADS, N_JT, BF16_ROWS, tb)
    key_tiles = (PEER_HEADS, N_ROWV, SUBLANES, tb)
    assert PEER_M == SUBLANES
    return pl.pallas_call(
        _peer_kernel,
        grid=(n // tb, n_chunks),
        in_specs=[
            pl.BlockSpec((tb, D_MODEL), tok, pipeline_mode=pl.Buffered(1)),
            pl.BlockSpec((1, D_MODEL), const2),
            pl.BlockSpec((2 * PEER_HEADS * PEER_D_KEY, D_MODEL), const2, pipeline_mode=pl.Buffered(1)),
            pl.BlockSpec((2 * PEER_HEADS, PEER_N_KEYS, PEER_D_KEY), lambda t, c: (0, 0, 0)),
            pl.BlockSpec((ec, D_MODEL), lambda t, c: (c, 0)),
            pl.BlockSpec((D_MODEL, ec), lambda t, c: (0, c)),
            pl.BlockSpec((1, D_MODEL), const2),
        ],
        out_specs=pl.BlockSpec((tb, D_MODEL), tok),
        out_shape=jax.ShapeDtypeStruct((n, D_MODEL), F32),
        scratch_shapes=[
            pltpu.VMEM((D_MODEL, tb), BF16),
            pltpu.VMEM((2 * PEER_HEADS * PEER_D_KEY, tb), BF16),
            pltpu.VMEM(head_tiles, BF16),
            pltpu.VMEM(head_tiles, BF16),
            pltpu.VMEM(key_tiles, F32),
            pltpu.VMEM(key_tiles, F32),
            pltpu.VMEM((PEER_N_KEYS, tb), F32),
            pltpu.VMEM((PEER_N_KEYS, tb), F32),
            pltpu.VMEM((D_MODEL, tb), F32),
        ],
        compiler_params=_cparams(("parallel", "arbitrary")),
        name="peer",
    )(x1, fw, wq_t, sub_keys, exp_u, exp_vt, nw)


def _lane_vec(two_by_heads):
    v = two_by_heads.astype(F32).reshape(1, 2 * SSD_HEADS)
    return jnp.pad(v, ((0, 0), (0, LANES - 2 * SSD_HEADS)))


def _trunk(x, p):
    b, l, d = x.shape
    n = b * l
    xf = x.reshape(n, d)
    q, k, v, z, xbc, dt = _inproj(xf, p["norm_mix_w"], p["w_main"], p["w_dt"])
    att = _attention(q.reshape(b, l, -1), k.reshape(b, l, -1), v.reshape(b, l, -1), p["att_bias"])
    xbc_act = _conv_silu(xbc.reshape(b, l, -1), p["conv_w"], p["conv_b"])
    yf, yb = _ssd_scan(xbc_act, dt.reshape(b, l, -1), p["dt_bias_vec"], p["a_log_vec"], p["tri"], p["expand"])
    x1 = _outproj(xf, att.reshape(n, -1), xbc_act.reshape(n, -1), yf.reshape(n, -1), yb.reshape(n, -1),
                  z, p["d_x"], p["ssd_norm_w"], p["w_out"])
    y = _peer(x1, p["norm_ffn_w"], p["wq_t"], p["sub_keys"], p["exp_u"], p["exp_vt"], p["norm_final_w"])
    return y.reshape(b, l, d)


def kernel(x_prompt, x_sample, norm_mix_w, w_in, rpb, conv_w, conv_b, dt_bias, a_log, d_skip, ssd_norm_w,
           w_out, norm_ffn_w, w_query, sub_keys, expert_u, expert_v, norm_final_w):
    assert w_in.shape[0] == 1, "single layer"
    w_in0 = w_in[0]
    w_dt = w_in0[:, MAIN_COLS:]
    w_dt = jnp.concatenate([w_dt, w_dt, jnp.zeros((D_MODEL, LANES - 2 * SSD_HEADS), w_dt.dtype)], axis=1)
    r = jnp.arange(Q)
    tril = (r[None, :] <= r[:, None])
    tri = jnp.stack([tril, tril.T]).astype(BF16)
    hid = jnp.arange(SSD_WIDTH) // SSD_HEAD_DIM
    lane = jnp.arange(LANES)
    expand = jnp.stack([lane[:, None] == hid[None, :],
                        lane[:, None] == hid[None, :] + SSD_HEADS]).astype(BF16)
    p = {
        "norm_mix_w": norm_mix_w[0].reshape(1, -1),
        "w_main": w_in0[:, :MAIN_COLS].astype(BF16),
        "w_dt": w_dt.astype(BF16),
        "att_bias": _attn_bias(rpb[0]),
        "conv_w": conv_w[0],
        "conv_b": conv_b[0].reshape(1, -1),
        "dt_bias_vec": _lane_vec(dt_bias[0]),
        "a_log_vec": _lane_vec(a_log[0]),
        "tri": tri,
        "expand": expand,
        "d_x": jnp.repeat(d_skip[0].astype(F32), SSD_HEAD_DIM).reshape(1, -1),
        "ssd_norm_w": ssd_norm_w[0].reshape(1, -1),
        "w_out": w_out[0].astype(BF16),
        "norm_ffn_w": norm_ffn_w[0].reshape(1, -1),
        "wq_t": w_query[0].T.astype(BF16),
        "sub_keys": sub_keys[0].reshape(2 * PEER_HEADS, PEER_N_KEYS, PEER_D_KEY).astype(BF16),
        "exp_u": expert_u[0].astype(BF16),
        "exp_vt": expert_v[0].T.astype(BF16),
        "norm_final_w": norm_final_w.reshape(1, -1),
    }
    return (_trunk(x_prompt, p), _trunk(x_sample, p))
```

```python
import functools
import math

import jax
import jax.numpy as jnp
import numpy as np
from jax import lax
from jax.experimental import pallas as pl
from jax.experimental.pallas import tpu as pltpu

F32 = jnp.float32
BF16 = jnp.bfloat16

D_MODEL = 1024
GRID_W = 64
ATT_HEADS = 8
ATT_HEAD_DIM = 64
ATT_WIDTH = 512
WIN_H = 8
WIN_W = 16
SSD_HEADS = 8
SSD_HEAD_DIM = 64
SSD_WIDTH = 512
SSD_GROUPS = 2
SSD_STATE = 128
SSD_CONV = 5
SSD_CHUNK = 128
CONV_CH = 1024
MAIN_COLS = 3 * ATT_WIDTH + SSD_WIDTH + CONV_CH
PEER_HEADS = 8
PEER_N_KEYS = 128
PEER_N_EXPERTS = PEER_N_KEYS * PEER_N_KEYS
PEER_D_KEY = 128
PEER_TOPK = 16
EPS = 1e-6

LANES = 128
VMEM_LIMIT = 60 * 1024 * 1024
NEG_BIG = -1e30
NT_DIMS = (((1,), (1,)), ((), ()))


def _cparams(sem):
    return pltpu.CompilerParams(dimension_semantics=sem, vmem_limit_bytes=VMEM_LIMIT)


def _split3(v):
    hi = v.astype(BF16)
    r1 = v - hi.astype(F32)
    mid = r1.astype(BF16)
    lo = (r1 - mid.astype(F32)).astype(BF16)
    return hi, mid, lo


IN_TM = 1024


def _inproj_kernel(x_ref, nw_ref, w_ref, wdt_ref, q_ref, k_ref, v_ref, z_ref, xbc_ref, dt_ref):
    x = x_ref[...]
    ms = jnp.mean(x * x, axis=-1, keepdims=True)
    h = (x * lax.rsqrt(ms + EPS) * nw_ref[...]).astype(BF16)

    def proj(lo, hi):
        return jnp.dot(h, w_ref[:, lo:hi], preferred_element_type=F32)

    q_ref[...] = proj(0, 512).astype(BF16)
    k_ref[...] = proj(512, 1024).astype(BF16)
    v_ref[...] = proj(1024, 1536).astype(BF16)
    z_ref[...] = proj(1536, 2048).astype(BF16)
    xbc_ref[...] = proj(2048, 3072)
    dt_ref[...] = jnp.dot(h, wdt_ref[...], preferred_element_type=F32)


def _inproj(x, nw, w_main, w_dt):
    n = x.shape[0]
    tm = IN_TM
    row = lambda i: (i, 0)
    full = lambda i: (0, 0)
    return pl.pallas_call(
        _inproj_kernel,
        grid=(n // tm,),
        in_specs=[
            pl.BlockSpec((tm, D_MODEL), row),
            pl.BlockSpec((1, D_MODEL), full),
            pl.BlockSpec((D_MODEL, MAIN_COLS), full),
            pl.BlockSpec((D_MODEL, LANES), full),
        ],
        out_specs=[
            pl.BlockSpec((tm, 512), row),
            pl.BlockSpec((tm, 512), row),
            pl.BlockSpec((tm, 512), row),
            pl.BlockSpec((tm, 512), row),
            pl.BlockSpec((tm, 1024), row),
            pl.BlockSpec((tm, LANES), row),
        ],
        out_shape=[
            jax.ShapeDtypeStruct((n, 512), BF16),
            jax.ShapeDtypeStruct((n, 512), BF16),
            jax.ShapeDtypeStruct((n, 512), BF16),
            jax.ShapeDtypeStruct((n, 512), BF16),
            jax.ShapeDtypeStruct((n, 1024), F32),
            jax.ShapeDtypeStruct((n, LANES), F32),
        ],
        compiler_params=_cparams(("parallel",)),
        name="inproj",
    )(x, nw, w_main, w_dt)


ATT_RB = 8
ATT_TOK = ATT_RB * GRID_W
ATT_HALO = WIN_H // 2
ATT_UROWS = ATT_RB + 2 * ATT_HALO
ATT_KEYS = ATT_UROWS * GRID_W
ATT_VARIANTS = 3
ATT_SUB = 4
ATT_PAIRS = 2
ATT_SUB_TOK = ATT_SUB * GRID_W
ATT_SUB_KEYS = (ATT_SUB + 2 * ATT_HALO) * GRID_W


def _attn_kernel(q_ref, kp_ref, kc_ref, kn_ref, vp_ref, vc_ref, vn_ref, bias_ref, o_ref):
    rb = pl.program_id(2)
    nrb = pl.num_programs(2)
    variant = jnp.where(rb == 0, 0, jnp.where(rb == nrb - 1, 2, 1))
    halo = ATT_HALO * GRID_W
    kw = jnp.concatenate([kp_ref[ATT_TOK - halo:, :], kc_ref[...], kn_ref[0:halo, :]], axis=0)
    vw = jnp.concatenate([vp_ref[ATT_TOK - halo:, :], vc_ref[...], vn_ref[0:halo, :]], axis=0)
    lane = lax.broadcasted_iota(jnp.int32, (ATT_SUB_TOK, LANES), 1)
    scale = jnp.asarray(ATT_HEAD_DIM ** -0.5, BF16)
    for pr in range(ATT_PAIRS):
        cols = slice(pr * LANES, (pr + 1) * LANES)
        for sb in range(ATT_RB // ATT_SUB):
            q = q_ref[sb * ATT_SUB_TOK:(sb + 1) * ATT_SUB_TOK, cols] * scale
            ks = kw[sb * ATT_SUB_TOK:sb * ATT_SUB_TOK + ATT_SUB_KEYS, cols]
            vs = vw[sb * ATT_SUB_TOK:sb * ATT_SUB_TOK + ATT_SUB_KEYS, cols]
            acc = jnp.zeros((ATT_SUB_TOK, LANES), F32)
            for hh in range(2):
                inh = (lane >= hh * ATT_HEAD_DIM) & (lane < (hh + 1) * ATT_HEAD_DIM)
                qm = jnp.where(inh, q, jnp.zeros_like(q))
                s = lax.dot_general(qm, ks, NT_DIMS, preferred_element_type=F32)
                s = s + bias_ref[2 * pr + hh, variant, sb]
                m = jnp.max(s, axis=-1, keepdims=True)
                p = jnp.exp(s - m)
                l = jnp.sum(p, axis=-1, keepdims=True)
                o = jnp.dot(p.astype(BF16), vs, preferred_element_type=F32) / l
                acc = jnp.where(inh, o, acc)
            o_ref[sb * ATT_SUB_TOK:(sb + 1) * ATT_SUB_TOK, cols] = acc.astype(BF16)


def _attn_bias(rpb):
    c = np.arange(GRID_W)
    col_start = np.clip(c - WIN_W // 2, 0, GRID_W - WIN_W)
    col_in = (c[None, :] >= col_start[:, None]) & (c[None, :] < col_start[:, None] + WIN_W)
    dc = np.clip(c[None, :] - c[:, None], -(WIN_W - 1), WIN_W - 1) + (WIN_W - 1)
    t = jnp.where(col_in[None, None], rpb.astype(F32)[:, :, dc], NEG_BIG)
    t = jnp.transpose(t, (0, 2, 1, 3)).reshape(ATT_HEADS, GRID_W, (2 * WIN_H - 1) * GRID_W)
    sub_rows = ATT_SUB + 2 * ATT_HALO
    n_sub = ATT_RB // ATT_SUB
    base = jnp.stack([t[:, :, (ATT_SUB - 1 - r) * GRID_W:(ATT_SUB - 1 - r + sub_rows) * GRID_W]
                      for _ in range(n_sub) for r in range(ATT_SUB)], axis=1)
    base = base.reshape(ATT_HEADS, 1, n_sub, ATT_SUB_TOK, ATT_SUB_KEYS)
    qr = np.arange(ATT_RB)[:, None]
    u = (qr // ATT_SUB) * ATT_SUB + np.arange(sub_rows)[None, :]
    first = np.maximum(qr - ATT_HALO, 0) + ATT_HALO
    last = np.minimum(qr - ATT_HALO, 0) + ATT_HALO
    starts = np.stack([np.broadcast_to(s, u.shape) for s in (first, qr, last)])
    valid = (u[None] >= starts) & (u[None] < starts + WIN_H)
    valid = np.repeat(np.repeat(valid, GRID_W, axis=1), GRID_W, axis=2)
    valid = valid.reshape(1, ATT_VARIANTS, n_sub, ATT_SUB_TOK, ATT_SUB_KEYS)
    return jnp.where(valid, base, NEG_BIG)


def _attention(q, k, v, bias):
    b, l, _ = q.shape
    rows = l // GRID_W
    assert rows % ATT_RB == 0 and rows // ATT_RB >= 2
    nrb = rows // ATT_RB
    cur = lambda hp, bi, rb: (bi, rb, hp)
    prv = lambda hp, bi, rb: (bi, jnp.maximum(rb - 1, 0), hp)
    nxt = lambda hp, bi, rb: (bi, jnp.minimum(rb + 1, nrb - 1), hp)
    blk = (None, ATT_TOK, ATT_PAIRS * LANES)
    return pl.pallas_call(
        _attn_kernel,
        grid=(ATT_HEADS // (2 * ATT_PAIRS), b, nrb),
        in_specs=[
            pl.BlockSpec(blk, cur),
            pl.BlockSpec(blk, prv), pl.BlockSpec(blk, cur), pl.BlockSpec(blk, nxt),
            pl.BlockSpec(blk, prv), pl.BlockSpec(blk, cur), pl.BlockSpec(blk, nxt),
            pl.BlockSpec((2 * ATT_PAIRS, ATT_VARIANTS, ATT_RB // ATT_SUB, ATT_SUB_TOK, ATT_SUB_KEYS),
                         lambda hp, bi, rb: (hp, 0, 0, 0, 0),
                         pipeline_mode=pl.Buffered(1)),
        ],
        out_specs=pl.BlockSpec(blk, cur),
        out_shape=jax.ShapeDtypeStruct((b, l, ATT_WIDTH), BF16),
        compiler_params=_cparams(("parallel", "parallel", "parallel")),
        name="nbr_attention",
    )(q, k, k, k, v, v, v, bias)


CONV_TL = 512
HALO = 8


def _conv_kernel(xp_ref, xc_ref, xn_ref, w_ref, b_ref, o_ref, buf):
    i = pl.program_id(1)
    nblk = pl.num_programs(1)
    prev = xp_ref[...]
    nxt = xn_ref[...]
    buf[0:HALO] = jnp.where(i == 0, jnp.zeros_like(prev), prev)
    buf[HALO:HALO + CONV_TL] = xc_ref[...]
    buf[HALO + CONV_TL:2 * HALO + CONV_TL] = jnp.where(i == nblk - 1, jnp.zeros_like(nxt), nxt)
    pad = SSD_CONV // 2
    y = jnp.zeros((CONV_TL, CONV_CH), F32) + b_ref[...]
    for j in range(SSD_CONV):
        y = y + buf[HALO - pad + j:HALO - pad + j + CONV_TL, :] * w_ref[j:j + 1, :]
    o_ref[...] = y * (1.0 / (1.0 + jnp.exp(-y)))


def _conv_silu(xbc, conv_w, conv_b):
    b, l, ch = xbc.shape
    tl = CONV_TL
    nblk = l // tl
    per = tl // HALO
    nh = l // HALO
    return pl.pallas_call(
        _conv_kernel,
        grid=(b, nblk),
        in_specs=[
            pl.BlockSpec((None, HALO, ch), lambda bi, i: (bi, jnp.maximum(i * per - 1, 0), 0)),
            pl.BlockSpec((None, tl, ch), lambda bi, i: (bi, i, 0)),
            pl.BlockSpec((None, HALO, ch), lambda bi, i: (bi, jnp.minimum((i + 1) * per, nh - 1), 0)),
            pl.BlockSpec((SSD_CONV, ch), lambda bi, i: (0, 0)),
            pl.BlockSpec((1, ch), lambda bi, i: (0, 0)),
        ],
        out_specs=pl.BlockSpec((None, tl, ch), lambda bi, i: (bi, i, 0)),
        out_shape=jax.ShapeDtypeStruct((b, l, ch), F32),
        scratch_shapes=[pltpu.VMEM((tl + 2 * HALO, ch), F32)],
        compiler_params=_cparams(("parallel", "parallel")),
        name="conv_silu",
    )(xbc, xbc, xbc, conv_w, conv_b)


Q = SSD_CHUNK
SSD_CPS = 4


def _ssd_direction(xs_ref, bm_ref, cm_ref, dt_ref, bias, a_vec, tri, expand_m, y_ref, s_ref, fwd, rows):
    xx = dt_ref[rows, :] + bias
    dtv = jnp.maximum(xx, 0.0) + jnp.log1p(jnp.exp(-jnp.abs(xx)))
    adt = dtv * a_vec
    hi, mid, lo = _split3(adt)
    cs3 = jnp.dot(tri, jnp.concatenate([hi, mid, lo], axis=1), preferred_element_type=F32)
    cs = cs3[:, 0:LANES] + cs3[:, LANES:2 * LANES] + cs3[:, 2 * LANES:3 * LANES]

    def expand(v):
        h3, m3, _ = _split3(v)
        r = jnp.dot(jnp.concatenate([h3, m3], axis=0), expand_m, preferred_element_type=F32)
        return r[0:Q] + r[Q:2 * Q]

    dt_x = expand(dtv)
    cs_x = expand(cs)
    last = Q - 1 if fwd else 0
    tot_x = cs_x[last:last + 1, :]
    xdt = xs_ref[rows, :] * dt_x
    xdt_b = xdt.astype(BF16)
    xw = (xdt * jnp.exp(tot_x - cs_x)).astype(BF16)
    scale_off = jnp.exp(cs_x)
    chunk_decay = jnp.exp(tot_x)
    cs_t = cs.T
    row = lax.broadcasted_iota(jnp.int32, (Q, Q), 0)
    col = lax.broadcasted_iota(jnp.int32, (Q, Q), 1)
    tri_mask = (col <= row) if fwd else (col >= row)
    gw = SSD_WIDTH // SSD_GROUPS
    for g in range(SSD_GROUPS):
        bg = bm_ref[rows, g * SSD_STATE:(g + 1) * SSD_STATE]
        cg = cm_ref[rows, g * SSD_STATE:(g + 1) * SSD_STATE].astype(BF16)
        cb = lax.dot_general(cg, bg.astype(BF16), NT_DIMS, preferred_element_type=F32)
        state = s_ref[g]
        yoff = jnp.dot(cg, state.astype(BF16), preferred_element_type=F32) * scale_off[:, g * gw:(g + 1) * gw]
        bg_t = bg.T.astype(BF16)
        s_ref[g] = state * chunk_decay[:, g * gw:(g + 1) * gw] + jnp.dot(
            bg_t, xw[:, g * gw:(g + 1) * gw], preferred_element_type=F32)
        for pp in range(2):
            pair = 2 * g + pp
            xpair = xdt_b[:, pair * LANES:(pair + 1) * LANES]
            ypair = yoff[:, pp * LANES:(pp + 1) * LANES]
            for hh in range(2):
                hc = 2 * pair + hh + (0 if fwd else SSD_HEADS)
                diff = cs[:, hc:hc + 1] - cs_t[hc:hc + 1, :]
                lm = jnp.where(tri_mask, jnp.exp(jnp.minimum(diff, 0.0)), 0.0)
                mm = (cb * lm).astype(BF16)
                inh = (col >= hh * SSD_HEAD_DIM) & (col < (hh + 1) * SSD_HEAD_DIM)
                ypair = ypair + jnp.dot(mm, jnp.where(inh, xpair, jnp.zeros_like(xpair)),
                                        preferred_element_type=F32)
            y_ref[rows, pair * LANES:(pair + 1) * LANES] = ypair.astype(y_ref.dtype)


def _ssd_kernel(xs_f, bm_f, cm_f, dt_f, xs_b, bm_b, cm_b, dt_b, bias_ref, alog_ref, tri_ref, e_ref,
                yf_ref, yb_ref, sf_ref, sb_ref):
    @pl.when(pl.program_id(1) == 0)
    def _():
        sf_ref[...] = jnp.zeros_like(sf_ref)
        sb_ref[...] = jnp.zeros_like(sb_ref)

    lane = lax.broadcasted_iota(jnp.int32, (1, LANES), 1)
    bias = bias_ref[...]
    a_vec = jnp.where(lane < 2 * SSD_HEADS, -jnp.exp(alog_ref[...]), 0.0)
    for k in range(SSD_CPS):
        up = slice(k * Q, (k + 1) * Q)
        down = slice((SSD_CPS - 1 - k) * Q, (SSD_CPS - k) * Q)
        _ssd_direction(xs_f, bm_f, cm_f, dt_f, bias, a_vec, tri_ref[0], e_ref[0], yf_ref, sf_ref, True, up)
        _ssd_direction(xs_b, bm_b, cm_b, dt_b, bias, a_vec, tri_ref[1], e_ref[1], yb_ref, sb_ref, False, down)


def _ssd_scan(xbc_act, dt, bias_vec, alog_vec, tri, expand_m):
    b, l, _ = xbc_act.shape
    nc = l // (Q * SSD_CPS)
    blk = Q * SSD_CPS
    fw = lambda cb: (lambda bi, c: (bi, c, cb))
    bw = lambda cb: (lambda bi, c: (bi, nc - 1 - c, cb))
    const2 = lambda bi, c: (0, 0)
    const3 = lambda bi, c: (0, 0, 0)

    def chunk_specs(mk):
        return [
            pl.BlockSpec((None, blk, SSD_WIDTH), mk(0)),
            pl.BlockSpec((None, blk, 2 * SSD_STATE), mk(2)),
            pl.BlockSpec((None, blk, 2 * SSD_STATE), mk(3)),
            pl.BlockSpec((None, blk, LANES), mk(0)),
        ]

    return pl.pallas_call(
        _ssd_kernel,
        grid=(b, nc),
        in_specs=chunk_specs(fw) + chunk_specs(bw) + [
            pl.BlockSpec((1, LANES), const2),
            pl.BlockSpec((1, LANES), const2),
            pl.BlockSpec((2, Q, Q), const3),
            pl.BlockSpec((2, LANES, SSD_WIDTH), const3),
        ],
        out_specs=[
            pl.BlockSpec((None, blk, SSD_WIDTH), fw(0)),
            pl.BlockSpec((None, blk, SSD_WIDTH), bw(0)),
        ],
        out_shape=[jax.ShapeDtypeStruct((b, l, SSD_WIDTH), BF16)] * 2,
        scratch_shapes=[pltpu.VMEM((SSD_GROUPS, SSD_STATE, 256), F32),
                        pltpu.VMEM((SSD_GROUPS, SSD_STATE, 256), F32)],
        compiler_params=_cparams(("parallel", "arbitrary")),
        name="ssd_scan",
    )(xbc_act, xbc_act, xbc_act, dt, xbc_act, xbc_act, xbc_act, dt, bias_vec, alog_vec, tri, expand_m)


OUT_TM = 1024


def _outproj_kernel(x_ref, att_ref, xs_ref, yf_ref, yb_ref, z_ref, dx_ref, snw_ref, wo_ref, x1_ref):
    z = z_ref[...].astype(F32)
    y = xs_ref[...] * dx_ref[...] + yf_ref[...].astype(F32) + yb_ref[...].astype(F32)
    y = y * (z * (1.0 / (1.0 + jnp.exp(-z))))
    ms = jnp.mean(y * y, axis=-1, keepdims=True)
    ssd = (y * lax.rsqrt(ms + EPS) * snw_ref[...]).astype(BF16)
    mixed = jnp.dot(att_ref[...], wo_ref[0:ATT_WIDTH, :], preferred_element_type=F32)
    mixed = mixed + jnp.dot(ssd, wo_ref[ATT_WIDTH:, :], preferred_element_type=F32)
    x1_ref[...] = x_ref[...] + mixed


def _outproj(x, att, xbc_act, yf, yb, z, d_x, snw, w_out):
    n = x.shape[0]
    tm = OUT_TM
    row = lambda i: (i, 0)
    full = lambda i: (0, 0)
    return pl.pallas_call(
        _outproj_kernel,
        grid=(n // tm,),
        in_specs=[
            pl.BlockSpec((tm, D_MODEL), row),
            pl.BlockSpec((tm, ATT_WIDTH), row),
            pl.BlockSpec((tm, SSD_WIDTH), row),
            pl.BlockSpec((tm, SSD_WIDTH), row),
            pl.BlockSpec((tm, SSD_WIDTH), row),
            pl.BlockSpec((tm, SSD_WIDTH), row),
            pl.BlockSpec((1, SSD_WIDTH), full),
            pl.BlockSpec((1, SSD_WIDTH), full),
            pl.BlockSpec((D_MODEL, D_MODEL), full),
        ],
        out_specs=pl.BlockSpec((tm, D_MODEL), row),
        out_shape=jax.ShapeDtypeStruct((n, D_MODEL), F32),
        compiler_params=_cparams(("parallel",)),
        name="outproj",
    )(x, att, xbc_act, yf, yb, z, d_x, snw, w_out)


PEER_TB = 1024
PEER_PIECE = 1024
PEER_EC = 1024
PEER_M = PEER_EC // PEER_N_KEYS
XU_PARTS = 4
BF16_ROWS = 16
N_JT = PEER_N_KEYS // BF16_ROWS


SUBLANES = 8
N_ROWV = PEER_N_KEYS // SUBLANES


def _oddeven_merge_sort_pairs(n):
    out, p = [], 1
    while p < n:
        k = p
        while k >= 1:
            for j in range(k % p, n - k, 2 * k):
                for i in range(min(k, n - j - k)):
                    if (i + j) // (2 * p) == (i + j + k) // (2 * p):
                        out.append((i + j, i + j + k))
            k //= 2
        p *= 2
    return out


SORT16_PAIRS = _oddeven_merge_sort_pairs(PEER_TOPK)


def _compare_exchange(rows, i, j):
    a, b = rows[i], rows[j]
    rows[i] = jnp.maximum(a, b)
    rows[j] = jnp.minimum(a, b)


def _sort_desc(rows):
    for i, j in SORT16_PAIRS:
        if j < len(rows):
            _compare_exchange(rows, i, j)


def _top16_over_sublanes(rows):
    rows = list(rows) + [None] * (PEER_TOPK - len(rows))
    for shift in (4, 2, 1):
        other = [None if r is None else pltpu.roll(r, shift, 0) for r in rows]
        merged = []
        for k in range(PEER_TOPK):
            a, b = rows[k], other[PEER_TOPK - 1 - k]
            merged.append(b if a is None else a if b is None else jnp.maximum(a, b))
        rows = merged
        for stride in (8, 4, 2, 1):
            for k in range(PEER_TOPK):
                if k & stride == 0:
                    _compare_exchange(rows, k, k + stride)
    return rows


def _sum_over_sublanes(x):
    for shift in (4, 2, 1):
        x = x + pltpu.roll(x, shift, 0)
    return x


def _peer_select_tile(s0_s, s1_s, lanes, lt, h, r1_s, e1_s, n_s, c_s):
    rowsl = lambda ref, k: ref[k * SUBLANES:(k + 1) * SUBLANES, lanes]
    s0 = [rowsl(s0_s, k) for k in range(N_ROWV)]
    s1 = [rowsl(s1_s, k) for k in range(N_ROWV)]
    v0, v1 = list(s0), list(s1)
    _sort_desc(v0)
    _sort_desc(v1)
    v0 = _top16_over_sublanes(v0)
    v1 = _top16_over_sublanes(v1)
    sub = lax.broadcasted_iota(jnp.int32, (SUBLANES, LANES), 0)

    def pack(vals):
        out = vals[0]
        for s in range(1, SUBLANES):
            out = jnp.where(sub == s, vals[s], out)
        return out

    v1_lo, v1_hi, v0_hi = pack(v1[0:8]), pack(v1[8:16]), pack(v0[8:16])
    cand = [v0[0] + v1_lo, v0[0] + v1_hi] + [v0[a] + v1_lo for a in range(1, 8)] + [v0_hi + v1[0]]
    csort = list(cand)
    _sort_desc(csort)
    csort = _top16_over_sublanes(csort)
    tau, cmax = csort[PEER_TOPK - 1], csort[0]
    zsum = jnp.zeros_like(tau)
    for cv in cand:
        zsum = zsum + jnp.where(cv >= tau, jnp.exp(cv - cmax), 0.0)
    rz = 1.0 / _sum_over_sublanes(zsum)
    hit = lambda x: jnp.where(x >= tau, 1.0, 0.0)
    n_a = []
    for a in range(PEER_TOPK):
        reach = PEER_TOPK // (a + 1)
        if reach > SUBLANES:
            n_a.append(_sum_over_sublanes(hit(v0[a] + v1_lo) + hit(v0[a] + v1_hi)))
        elif reach > 3:
            n_a.append(_sum_over_sublanes(hit(v0[a] + v1_lo)))
        else:
            n_a.append(sum(hit(v0[a] + v1[b]) for b in range(reach)))
    for m in range(N_JT):
        packed_r, packed_e = [], []
        for k in (2 * m, 2 * m + 1):
            n = jnp.zeros_like(tau)
            r = jnp.full_like(tau, float(PEER_TOPK))
            for a in range(PEER_TOPK):
                n = jnp.where(s0[k] == v0[a], n_a[a], n)
                r = jnp.where(s1[k] == v1[a], float(a), r)
            n_s[h, k, lt] = n
            c_s[h, k, lt] = jnp.exp(s0[k] - v0[0]) * rz
            packed_r.append(r)
            packed_e.append(jnp.exp(s1[k] - v1[0]))
        r1_s[h, m, :, lanes] = jnp.concatenate(packed_r, axis=0).astype(BF16)
        e1_s[h, m, :, lanes] = jnp.concatenate(packed_e, axis=0).astype(BF16)


def _peer_kernel(x1_ref, fw_ref, wqt_ref, sk_ref, u_ref, vt_ref, nw_ref, o_ref,
                 hf_s, qt_s, r1_s, e1_s, n_s, c_s, s0_s, s1_s, acc_s):
    c = pl.program_id(1)

    @pl.when(c == 0)
    def _prepare():
        x1 = x1_ref[...]
        ms = jnp.mean(x1 * x1, axis=-1, keepdims=True)
        hf_s[...] = (x1 * lax.rsqrt(ms + EPS) * fw_ref[...]).T.astype(BF16)
        qt_s[...] = jnp.dot(wqt_ref[...], hf_s[...], preferred_element_type=F32).astype(BF16)
        acc_s[...] = jnp.zeros_like(acc_s)

        def per_head(h, carry):
            r0 = pl.multiple_of(h * 2 * PEER_D_KEY, 2 * PEER_D_KEY)
            s0 = jnp.dot(sk_ref[2 * h], qt_s[pl.ds(r0, PEER_D_KEY), :], preferred_element_type=F32)
            s1 = jnp.dot(sk_ref[2 * h + 1], qt_s[pl.ds(r0 + PEER_D_KEY, PEER_D_KEY), :],
                         preferred_element_type=F32)
            s0_s[...] = s0
            s1_s[...] = s1

            def per_tile(lt, carry2):
                lanes = pl.ds(pl.multiple_of(lt * LANES, LANES), LANES)
                _peer_select_tile(s0_s, s1_s, lanes, lt, h, r1_s, e1_s, n_s, c_s)
                return carry2

            lax.fori_loop(0, PEER_TB // LANES, per_tile, 0)
            return carry

        lax.fori_loop(0, PEER_HEADS, per_head, 0)

    n_pieces = PEER_TB // PEER_PIECE

    def expert_scores(p, part):
        rows = slice(part * (PEER_EC // XU_PARTS), (part + 1) * (PEER_EC // XU_PARTS))
        return jnp.dot(u_ref[rows, :], hf_s[:, p * PEER_PIECE:(p + 1) * PEER_PIECE],
                       preferred_element_type=F32)

    for p in range(n_pieces):
        lanes = slice(p * PEER_PIECE, (p + 1) * PEER_PIECE)
        a_parts = []
        for part in range(XU_PARTS):
            xb = expert_scores(p, part).astype(BF16)
            act = (0.5 * xb) * (1.0 + lax.erf(xb * (2.0 ** -0.5)))
            gates = []
            for ii in range(part * PEER_M // XU_PARTS, (part + 1) * PEER_M // XU_PARTS):
                g = jnp.zeros((N_JT, BF16_ROWS, PEER_PIECE), BF16)
                for h in range(PEER_HEADS):
                    tiles = range(lanes.start // LANES, lanes.stop // LANES)
                    nrow = jnp.concatenate([n_s[h, c, t, pl.ds(ii, SUBLANES, stride=0), :] for t in tiles], axis=1)
                    crow = jnp.concatenate([c_s[h, c, t, pl.ds(ii, SUBLANES, stride=0), :] for t in tiles], axis=1)
                    nb = jnp.concatenate([nrow, nrow], axis=0).astype(BF16)
                    cb = jnp.concatenate([crow, crow], axis=0).astype(BF16)
                    sel = jnp.where(r1_s[h, :, :, lanes] < nb[None], e1_s[h, :, :, lanes], jnp.zeros((), BF16))
                    g = g + sel * cb[None]
                gates.append(g.reshape(PEER_N_KEYS, PEER_PIECE))
            a_parts.append(act * jnp.concatenate(gates, axis=0))
        a_t = jnp.concatenate(a_parts, axis=0)
        acc_s[:, lanes] += jnp.dot(vt_ref[...], a_t, preferred_element_type=F32)

    @pl.when(c == pl.num_programs(1) - 1)
    def _finish():
        y = x1_ref[...] + acc_s[...].T
        ms = jnp.mean(y * y, axis=-1, keepdims=True)
        o_ref[...] = y * lax.rsqrt(ms + EPS) * nw_ref[...]


def _peer(x1, fw, wq_t, sub_keys, exp_u, exp_vt, nw):
    n = x1.shape[0]
    tb, ec = PEER_TB, PEER_EC
    n_chunks = PEER_N_EXPERTS // ec
    tok = lambda t, c: (t, 0)
    const2 = lambda t, c: (0, 0)
    head_tiles = (PEER_HEADS, N_JT, BF16_ROWS, tb)
    key_tiles = (PEER_HEADS, N_ROWV, tb // LANES, SUBLANES, LANES)
    assert PEER_M == SUBLANES
    return pl.pallas_call(
        _peer_kernel,
        grid=(n // tb, n_chunks),
        in_specs=[
            pl.BlockSpec((tb, D_MODEL), tok, pipeline_mode=pl.Buffered(1)),
            pl.BlockSpec((1, D_MODEL), const2),
            pl.BlockSpec((2 * PEER_HEADS * PEER_D_KEY, D_MODEL), const2, pipeline_mode=pl.Buffered(1)),
            pl.BlockSpec((2 * PEER_HEADS, PEER_N_KEYS, PEER_D_KEY), lambda t, c: (0, 0, 0)),
            pl.BlockSpec((ec, D_MODEL), lambda t, c: (c, 0)),
            pl.BlockSpec((D_MODEL, ec), lambda t, c: (0, c)),
            pl.BlockSpec((1, D_MODEL), const2),
        ],
        out_specs=pl.BlockSpec((tb, D_MODEL), tok),
        out_shape=jax.ShapeDtypeStruct((n, D_MODEL), F32),
        scratch_shapes=[
            pltpu.VMEM((D_MODEL, tb), BF16),
            pltpu.VMEM((2 * PEER_HEADS * PEER_D_KEY, tb), BF16),
            pltpu.VMEM(head_tiles, BF16),
            pltpu.VMEM(head_tiles, BF16),
            pltpu.VMEM(key_tiles, F32),
            pltpu.VMEM(key_tiles, F32),
            pltpu.VMEM((PEER_N_KEYS, tb), F32),
            pltpu.VMEM((PEER_N_KEYS, tb), F32),
            pltpu.VMEM((D_MODEL, tb), F32),
        ],
        compiler_params=_cparams(("parallel", "arbitrary")),
        name="peer",
    )(x1, fw, wq_t, sub_keys, exp_u, exp_vt, nw)


def _lane_vec(two_by_heads):
    v = two_by_heads.astype(F32).reshape(1, 2 * SSD_HEADS)
    return jnp.pad(v, ((0, 0), (0, LANES - 2 * SSD_HEADS)))


def _trunk(x, p):
    b, l, d = x.shape
    n = b * l
    xf = x.reshape(n, d)
    q, k, v, z, xbc, dt = _inproj(xf, p["norm_mix_w"], p["w_main"], p["w_dt"])
    att = _attention(q.reshape(b, l, -1), k.reshape(b, l, -1), v.reshape(b, l, -1), p["att_bias"])
    xbc_act = _conv_silu(xbc.reshape(b, l, -1), p["conv_w"], p["conv_b"])
    yf, yb = _ssd_scan(xbc_act, dt.reshape(b, l, -1), p["dt_bias_vec"], p["a_log_vec"], p["tri"], p["expand"])
    x1 = _outproj(xf, att.reshape(n, -1), xbc_act.reshape(n, -1), yf.reshape(n, -1), yb.reshape(n, -1),
                  z, p["d_x"], p["ssd_norm_w"], p["w_out"])
    y = _peer(x1, p["norm_ffn_w"], p["wq_t"], p["sub_keys"], p["exp_u"], p["exp_vt"], p["norm_final_w"])
    return y.reshape(b, l, d)


def kernel(x_prompt, x_sample, norm_mix_w, w_in, rpb, conv_w, conv_b, dt_bias, a_log, d_skip, ssd_norm_w,
           w_out, norm_ffn_w, w_query, sub_keys, expert_u, expert_v, norm_final_w):
    assert w_in.shape[0] == 1, "single layer"
    w_in0 = w_in[0]
    w_dt = w_in0[:, MAIN_COLS:]
    w_dt = jnp.concatenate([w_dt, w_dt, jnp.zeros((D_MODEL, LANES - 2 * SSD_HEADS), w_dt.dtype)], axis=1)
    r = jnp.arange(Q)
    tril = (r[None, :] <= r[:, None])
    tri = jnp.stack([tril, tril.T]).astype(BF16)
    hid = jnp.arange(SSD_WIDTH) // SSD_HEAD_DIM
    lane = jnp.arange(LANES)
    expand = jnp.stack([lane[:, None] == hid[None, :],
                        lane[:, None] == hid[None, :] + SSD_HEADS]).astype(BF16)
    p = {
        "norm_mix_w": norm_mix_w[0].reshape(1, -1),
        "w_main": w_in0[:, :MAIN_COLS].astype(BF16),
        "w_dt": w_dt.astype(BF16),
        "att_bias": _attn_bias(rpb[0]),
        "conv_w": conv_w[0],
        "conv_b": conv_b[0].reshape(1, -1),
        "dt_bias_vec": _lane_vec(dt_bias[0]),
        "a_log_vec": _lane_vec(a_log[0]),
        "tri": tri,
        "expand": expand,
        "d_x": jnp.repeat(d_skip[0].astype(F32), SSD_HEAD_DIM).reshape(1, -1),
        "ssd_norm_w": ssd_norm_w[0].reshape(1, -1),
        "w_out": w_out[0].astype(BF16),
        "norm_ffn_w": norm_ffn_w[0].reshape(1, -1),
        "wq_t": w_query[0].T.astype(BF16),
        "sub_keys": sub_keys[0].reshape(2 * PEER_HEADS, PEER_N_KEYS, PEER_D_KEY).astype(BF16),
        "exp_u": expert_u[0].astype(BF16),
        "exp_vt": expert_v[0].T.astype(BF16),
        "norm_final_w": norm_final_w.reshape(1, -1),
    }
    return (_trunk(x_prompt, p), _trunk(x_sample, p))
```

```python
import functools
import math

import jax
import jax.numpy as jnp
import numpy as np
from jax import lax
from jax.experimental import pallas as pl
from jax.experimental.pallas import tpu as pltpu

F32 = jnp.float32
BF16 = jnp.bfloat16

D_MODEL = 1024
GRID_W = 64
ATT_HEADS = 8
ATT_HEAD_DIM = 64
ATT_WIDTH = 512
WIN_H = 8
WIN_W = 16
SSD_HEADS = 8
SSD_HEAD_DIM = 64
SSD_WIDTH = 512
SSD_GROUPS = 2
SSD_STATE = 128
SSD_CONV = 5
SSD_CHUNK = 128
CONV_CH = 1024
MAIN_COLS = 3 * ATT_WIDTH + SSD_WIDTH + CONV_CH
PEER_HEADS = 8
PEER_N_KEYS = 128
PEER_N_EXPERTS = PEER_N_KEYS * PEER_N_KEYS
PEER_D_KEY = 128
PEER_TOPK = 16
EPS = 1e-6

LANES = 128
VMEM_LIMIT = 60 * 1024 * 1024
NEG_BIG = -1e30
NT_DIMS = (((1,), (1,)), ((), ()))


def _cparams(sem):
    return pltpu.CompilerParams(dimension_semantics=sem, vmem_limit_bytes=VMEM_LIMIT)


def _split3(v):
    hi = v.astype(BF16)
    r1 = v - hi.astype(F32)
    mid = r1.astype(BF16)
    lo = (r1 - mid.astype(F32)).astype(BF16)
    return hi, mid, lo


IN_TM = 1024


def _inproj_kernel(x_ref, nw_ref, w_ref, wdt_ref, q_ref, k_ref, v_ref, z_ref, xbc_ref, dt_ref):
    x = x_ref[...]
    ms = jnp.mean(x * x, axis=-1, keepdims=True)
    h = (x * lax.rsqrt(ms + EPS) * nw_ref[...]).astype(BF16)

    def proj(lo, hi):
        return jnp.dot(h, w_ref[:, lo:hi], preferred_element_type=F32)

    q_ref[...] = proj(0, 512).astype(BF16)
    k_ref[...] = proj(512, 1024).astype(BF16)
    v_ref[...] = proj(1024, 1536).astype(BF16)
    z_ref[...] = proj(1536, 2048).astype(BF16)
    xbc_ref[...] = proj(2048, 3072)
    dt_ref[...] = jnp.dot(h, wdt_ref[...], preferred_element_type=F32)


def _inproj(x, nw, w_main, w_dt):
    n = x.shape[0]
    tm = IN_TM
    row = lambda i: (i, 0)
    full = lambda i: (0, 0)
    return pl.pallas_call(
        _inproj_kernel,
        grid=(n // tm,),
        in_specs=[
            pl.BlockSpec((tm, D_MODEL), row),
            pl.BlockSpec((1, D_MODEL), full),
            pl.BlockSpec((D_MODEL, MAIN_COLS), full),
            pl.BlockSpec((D_MODEL, LANES), full),
        ],
        out_specs=[
            pl.BlockSpec((tm, 512), row),
            pl.BlockSpec((tm, 512), row),
            pl.BlockSpec((tm, 512), row),
            pl.BlockSpec((tm, 512), row),
            pl.BlockSpec((tm, 1024), row),
            pl.BlockSpec((tm, LANES), row),
        ],
        out_shape=[
            jax.ShapeDtypeStruct((n, 512), BF16),
            jax.ShapeDtypeStruct((n, 512), BF16),
            jax.ShapeDtypeStruct((n, 512), BF16),
            jax.ShapeDtypeStruct((n, 512), BF16),
            jax.ShapeDtypeStruct((n, 1024), F32),
            jax.ShapeDtypeStruct((n, LANES), F32),
        ],
        compiler_params=_cparams(("parallel",)),
        name="inproj",
    )(x, nw, w_main, w_dt)


ATT_RB = 8
ATT_TOK = ATT_RB * GRID_W
ATT_HALO = WIN_H // 2
ATT_UROWS = ATT_RB + 2 * ATT_HALO
ATT_KEYS = ATT_UROWS * GRID_W
ATT_VARIANTS = 3
ATT_SUB = 4
ATT_PAIRS = 2
ATT_SUB_TOK = ATT_SUB * GRID_W
ATT_SUB_KEYS = (ATT_SUB + 2 * ATT_HALO) * GRID_W


def _attn_kernel(q_ref, kp_ref, kc_ref, kn_ref, vp_ref, vc_ref, vn_ref, bias_ref, o_ref):
    rb = pl.program_id(2)
    nrb = pl.num_programs(2)
    variant = jnp.where(rb == 0, 0, jnp.where(rb == nrb - 1, 2, 1))
    halo = ATT_HALO * GRID_W
    kw = jnp.concatenate([kp_ref[ATT_TOK - halo:, :], kc_ref[...], kn_ref[0:halo, :]], axis=0)
    vw = jnp.concatenate([vp_ref[ATT_TOK - halo:, :], vc_ref[...], vn_ref[0:halo, :]], axis=0)
    lane = lax.broadcasted_iota(jnp.int32, (ATT_SUB_TOK, LANES), 1)
    scale = jnp.asarray(ATT_HEAD_DIM ** -0.5, BF16)
    for pr in range(ATT_PAIRS):
        cols = slice(pr * LANES, (pr + 1) * LANES)
        for sb in range(ATT_RB // ATT_SUB):
            q = q_ref[sb * ATT_SUB_TOK:(sb + 1) * ATT_SUB_TOK, cols] * scale
            ks = kw[sb * ATT_SUB_TOK:sb * ATT_SUB_TOK + ATT_SUB_KEYS, cols]
            vs = vw[sb * ATT_SUB_TOK:sb * ATT_SUB_TOK + ATT_SUB_KEYS, cols]
            acc = jnp.zeros((ATT_SUB_TOK, LANES), F32)
            for hh in range(2):
                inh = (lane >= hh * ATT_HEAD_DIM) & (lane < (hh + 1) * ATT_HEAD_DIM)
                qm = jnp.where(inh, q, jnp.zeros_like(q))
                s = lax.dot_general(qm, ks, NT_DIMS, preferred_element_type=F32)
                s = s + bias_ref[2 * pr + hh, variant, sb]
                m = jnp.max(s, axis=-1, keepdims=True)
                p = jnp.exp(s - m)
                l = jnp.sum(p, axis=-1, keepdims=True)
                o = jnp.dot(p.astype(BF16), vs, preferred_element_type=F32) / l
                acc = jnp.where(inh, o, acc)
            o_ref[sb * ATT_SUB_TOK:(sb + 1) * ATT_SUB_TOK, cols] = acc.astype(BF16)


def _attn_bias(rpb):
    c = np.arange(GRID_W)
    col_start = np.clip(c - WIN_W // 2, 0, GRID_W - WIN_W)
    col_in = (c[None, :] >= col_start[:, None]) & (c[None, :] < col_start[:, None] + WIN_W)
    n_dr, n_dc = 2 * WIN_H - 1, 2 * WIN_W - 1
    left = GRID_W - 1 - (WIN_W - 1)
    ext = jnp.pad(rpb.astype(F32), ((0, 0), (0, 0), (left, 2 * GRID_W - n_dc - left)))
    flat = jnp.broadcast_to(ext[:, :, None, :], (ATT_HEADS, n_dr, GRID_W, 2 * GRID_W)).reshape(ATT_HEADS, n_dr, -1)
    skew = flat[:, :, GRID_W - 1:GRID_W - 1 + GRID_W * (2 * GRID_W - 1)].reshape(ATT_HEADS, n_dr, GRID_W, 2 * GRID_W - 1)
    t = jnp.where(col_in[None, None], skew[:, :, :, :GRID_W], NEG_BIG)
    t = jnp.transpose(t, (0, 2, 1, 3)).reshape(ATT_HEADS, GRID_W, (2 * WIN_H - 1) * GRID_W)
    sub_rows = ATT_SUB + 2 * ATT_HALO
    n_sub = ATT_RB // ATT_SUB
    base = jnp.stack([t[:, :, (ATT_SUB - 1 - r) * GRID_W:(ATT_SUB - 1 - r + sub_rows) * GRID_W]
                      for _ in range(n_sub) for r in range(ATT_SUB)], axis=1)
    base = base.reshape(ATT_HEADS, 1, n_sub, ATT_SUB_TOK, ATT_SUB_KEYS)
    qr = np.arange(ATT_RB)[:, None]
    u = (qr // ATT_SUB) * ATT_SUB + np.arange(sub_rows)[None, :]
    first = np.maximum(qr - ATT_HALO, 0) + ATT_HALO
    last = np.minimum(qr - ATT_HALO, 0) + ATT_HALO
    starts = np.stack([np.broadcast_to(s, u.shape) for s in (first, qr, last)])
    valid = (u[None] >= starts) & (u[None] < starts + WIN_H)
    valid = np.repeat(np.repeat(valid, GRID_W, axis=1), GRID_W, axis=2)
    valid = valid.reshape(1, ATT_VARIANTS, n_sub, ATT_SUB_TOK, ATT_SUB_KEYS)
    return jnp.where(valid, base, NEG_BIG)


def _attention(q, k, v, bias):
    b, l, _ = q.shape
    rows = l // GRID_W
    assert rows % ATT_RB == 0 and rows // ATT_RB >= 2
    nrb = rows // ATT_RB
    cur = lambda hp, bi, rb: (bi, rb, hp)
    prv = lambda hp, bi, rb: (bi, jnp.maximum(rb - 1, 0), hp)
    nxt = lambda hp, bi, rb: (bi, jnp.minimum(rb + 1, nrb - 1), hp)
    blk = (None, ATT_TOK, ATT_PAIRS * LANES)
    return pl.pallas_call(
        _attn_kernel,
        grid=(ATT_HEADS // (2 * ATT_PAIRS), b, nrb),
        in_specs=[
            pl.BlockSpec(blk, cur),
            pl.BlockSpec(blk, prv), pl.BlockSpec(blk, cur), pl.BlockSpec(blk, nxt),
            pl.BlockSpec(blk, prv), pl.BlockSpec(blk, cur), pl.BlockSpec(blk, nxt),
            pl.BlockSpec((2 * ATT_PAIRS, ATT_VARIANTS, ATT_RB // ATT_SUB, ATT_SUB_TOK, ATT_SUB_KEYS),
                         lambda hp, bi, rb: (hp, 0, 0, 0, 0),
                         pipeline_mode=pl.Buffered(1)),
        ],
        out_specs=pl.BlockSpec(blk, cur),
        out_shape=jax.ShapeDtypeStruct((b, l, ATT_WIDTH), BF16),
        compiler_params=_cparams(("parallel", "parallel", "parallel")),
        name="nbr_attention",
    )(q, k, k, k, v, v, v, bias)


CONV_TL = 512
HALO = 8


def _conv_kernel(xp_ref, xc_ref, xn_ref, w_ref, b_ref, o_ref, buf):
    i = pl.program_id(1)
    nblk = pl.num_programs(1)
    prev = xp_ref[...]
    nxt = xn_ref[...]
    buf[0:HALO] = jnp.where(i == 0, jnp.zeros_like(prev), prev)
    buf[HALO:HALO + CONV_TL] = xc_ref[...]
    buf[HALO + CONV_TL:2 * HALO + CONV_TL] = jnp.where(i == nblk - 1, jnp.zeros_like(nxt), nxt)
    x = buf[...]
    rows = CONV_TL + 2 * HALO
    w = [w_ref[j:j + 1, :] for j in range(SSD_CONV)]
    up = lambda a: pltpu.roll(a, rows - 1, 0)
    down = lambda a: pltpu.roll(a, 1, 0)
    y = x * w[2] + up(up(x * w[4]) + x * w[3]) + down(down(x * w[0]) + x * w[1])
    y = y[HALO:HALO + CONV_TL, :] + b_ref[...]
    o_ref[...] = y * (1.0 / (1.0 + jnp.exp(-y)))


def _conv_silu(xbc, conv_w, conv_b):
    b, l, ch = xbc.shape
    tl = CONV_TL
    nblk = l // tl
    per = tl // HALO
    nh = l // HALO
    return pl.pallas_call(
        _conv_kernel,
        grid=(b, nblk),
        in_specs=[
            pl.BlockSpec((None, HALO, ch), lambda bi, i: (bi, jnp.maximum(i * per - 1, 0), 0)),
            pl.BlockSpec((None, tl, ch), lambda bi, i: (bi, i, 0)),
            pl.BlockSpec((None, HALO, ch), lambda bi, i: (bi, jnp.minimum((i + 1) * per, nh - 1), 0)),
            pl.BlockSpec((SSD_CONV, ch), lambda bi, i: (0, 0)),
            pl.BlockSpec((1, ch), lambda bi, i: (0, 0)),
        ],
        out_specs=pl.BlockSpec((None, tl, ch), lambda bi, i: (bi, i, 0)),
        out_shape=jax.ShapeDtypeStruct((b, l, ch), F32),
        scratch_shapes=[pltpu.VMEM((tl + 2 * HALO, ch), F32)],
        compiler_params=_cparams(("parallel", "parallel")),
        name="conv_silu",
    )(xbc, xbc, xbc, conv_w, conv_b)


Q = SSD_CHUNK
SSD_CPS = 4


def _ssd_direction(xs_ref, bm_ref, cm_ref, dt_ref, bias, a_vec, tri, expand_m, y_ref, s_ref, fwd, rows):
    xx = dt_ref[rows, :] + bias
    dtv = jnp.maximum(xx, 0.0) + jnp.log1p(jnp.exp(-jnp.abs(xx)))
    adt = dtv * a_vec
    hi, mid, lo = _split3(adt)
    cs3 = jnp.dot(tri, jnp.concatenate([hi, mid, lo], axis=1), preferred_element_type=F32)
    cs = cs3[:, 0:LANES] + cs3[:, LANES:2 * LANES] + cs3[:, 2 * LANES:3 * LANES]

    def expand(v):
        h3, m3, _ = _split3(v)
        r = jnp.dot(jnp.concatenate([h3, m3], axis=0), expand_m, preferred_element_type=F32)
        return r[0:Q] + r[Q:2 * Q]

    dt_x = expand(dtv)
    cs_x = expand(cs)
    last = Q - 1 if fwd else 0
    tot_x = cs_x[last:last + 1, :]
    xdt = xs_ref[rows, :] * dt_x
    xdt_b = xdt.astype(BF16)
    xw = (xdt * jnp.exp(tot_x - cs_x)).astype(BF16)
    scale_off = jnp.exp(cs_x)
    chunk_decay = jnp.exp(tot_x)
    cs_t = cs.T
    row = lax.broadcasted_iota(jnp.int32, (Q, Q), 0)
    col = lax.broadcasted_iota(jnp.int32, (Q, Q), 1)
    tri_mask = (col <= row) if fwd else (col >= row)
    gw = SSD_WIDTH // SSD_GROUPS
    for g in range(SSD_GROUPS):
        bg = bm_ref[rows, g * SSD_STATE:(g + 1) * SSD_STATE]
        cg = cm_ref[rows, g * SSD_STATE:(g + 1) * SSD_STATE].astype(BF16)
        cb = lax.dot_general(cg, bg.astype(BF16), NT_DIMS, preferred_element_type=F32)
        state = s_ref[g]
        yoff = jnp.dot(cg, state.astype(BF16), preferred_element_type=F32) * scale_off[:, g * gw:(g + 1) * gw]
        bg_t = bg.T.astype(BF16)
        s_ref[g] = state * chunk_decay[:, g * gw:(g + 1) * gw] + jnp.dot(
            bg_t, xw[:, g * gw:(g + 1) * gw], preferred_element_type=F32)
        for pp in range(2):
            pair = 2 * g + pp
            xpair = xdt_b[:, pair * LANES:(pair + 1) * LANES]
            ypair = yoff[:, pp * LANES:(pp + 1) * LANES]
            for hh in range(2):
                hc = 2 * pair + hh + (0 if fwd else SSD_HEADS)
                diff = cs[:, hc:hc + 1] - cs_t[hc:hc + 1, :]
                lm = jnp.where(tri_mask, jnp.exp(jnp.minimum(diff, 0.0)), 0.0)
                mm = (cb * lm).astype(BF16)
                inh = (col >= hh * SSD_HEAD_DIM) & (col < (hh + 1) * SSD_HEAD_DIM)
                ypair = ypair + jnp.dot(mm, jnp.where(inh, xpair, jnp.zeros_like(xpair)),
                                        preferred_element_type=F32)
            y_ref[rows, pair * LANES:(pair + 1) * LANES] = ypair.astype(y_ref.dtype)


def _ssd_kernel(xs_f, bm_f, cm_f, dt_f, xs_b, bm_b, cm_b, dt_b, bias_ref, alog_ref, tri_ref, e_ref,
                yf_ref, yb_ref, sf_ref, sb_ref):
    @pl.when(pl.program_id(1) == 0)
    def _():
        sf_ref[...] = jnp.zeros_like(sf_ref)
        sb_ref[...] = jnp.zeros_like(sb_ref)

    lane = lax.broadcasted_iota(jnp.int32, (1, LANES), 1)
    bias = bias_ref[...]
    a_vec = jnp.where(lane < 2 * SSD_HEADS, -jnp.exp(alog_ref[...]), 0.0)
    for k in range(SSD_CPS):
        up = slice(k * Q, (k + 1) * Q)
        down = slice((SSD_CPS - 1 - k) * Q, (SSD_CPS - k) * Q)
        _ssd_direction(xs_f, bm_f, cm_f, dt_f, bias, a_vec, tri_ref[0], e_ref[0], yf_ref, sf_ref, True, up)
        _ssd_direction(xs_b, bm_b, cm_b, dt_b, bias, a_vec, tri_ref[1], e_ref[1], yb_ref, sb_ref, False, down)


def _ssd_scan(xbc_act, dt, bias_vec, alog_vec, tri, expand_m):
    b, l, _ = xbc_act.shape
    nc = l // (Q * SSD_CPS)
    blk = Q * SSD_CPS
    fw = lambda cb: (lambda bi, c: (bi, c, cb))
    bw = lambda cb: (lambda bi, c: (bi, nc - 1 - c, cb))
    const2 = lambda bi, c: (0, 0)
    const3 = lambda bi, c: (0, 0, 0)

    def chunk_specs(mk):
        return [
            pl.BlockSpec((None, blk, SSD_WIDTH), mk(0)),
            pl.BlockSpec((None, blk, 2 * SSD_STATE), mk(2)),
            pl.BlockSpec((None, blk, 2 * SSD_STATE), mk(3)),
            pl.BlockSpec((None, blk, LANES), mk(0)),
        ]

    return pl.pallas_call(
        _ssd_kernel,
        grid=(b, nc),
        in_specs=chunk_specs(fw) + chunk_specs(bw) + [
            pl.BlockSpec((1, LANES), const2),
            pl.BlockSpec((1, LANES), const2),
            pl.BlockSpec((2, Q, Q), const3),
            pl.BlockSpec((2, LANES, SSD_WIDTH), const3),
        ],
        out_specs=[
            pl.BlockSpec((None, blk, SSD_WIDTH), fw(0)),
            pl.BlockSpec((None, blk, SSD_WIDTH), bw(0)),
        ],
        out_shape=[jax.ShapeDtypeStruct((b, l, SSD_WIDTH), BF16)] * 2,
        scratch_shapes=[pltpu.VMEM((SSD_GROUPS, SSD_STATE, 256), F32),
                        pltpu.VMEM((SSD_GROUPS, SSD_STATE, 256), F32)],
        compiler_params=_cparams(("parallel", "arbitrary")),
        name="ssd_scan",
    )(xbc_act, xbc_act, xbc_act, dt, xbc_act, xbc_act, xbc_act, dt, bias_vec, alog_vec, tri, expand_m)


OUT_TM = 1024


def _outproj_kernel(x_ref, att_ref, xs_ref, yf_ref, yb_ref, z_ref, dx_ref, snw_ref, wo_ref, x1_ref):
    z = z_ref[...].astype(F32)
    y = xs_ref[...] * dx_ref[...] + yf_ref[...].astype(F32) + yb_ref[...].astype(F32)
    y = y * (z * (1.0 / (1.0 + jnp.exp(-z))))
    ms = jnp.mean(y * y, axis=-1, keepdims=True)
    ssd = (y * lax.rsqrt(ms + EPS) * snw_ref[...]).astype(BF16)
    mixed = jnp.dot(att_ref[...], wo_ref[0:ATT_WIDTH, :], preferred_element_type=F32)
    mixed = mixed + jnp.dot(ssd, wo_ref[ATT_WIDTH:, :], preferred_element_type=F32)
    x1_ref[...] = x_ref[...] + mixed


def _outproj(x, att, xbc_act, yf, yb, z, d_x, snw, w_out):
    n = x.shape[0]
    tm = OUT_TM
    row = lambda i: (i, 0)
    full = lambda i: (0, 0)
    return pl.pallas_call(
        _outproj_kernel,
        grid=(n // tm,),
        in_specs=[
            pl.BlockSpec((tm, D_MODEL), row),
            pl.BlockSpec((tm, ATT_WIDTH), row),
            pl.BlockSpec((tm, SSD_WIDTH), row),
            pl.BlockSpec((tm, SSD_WIDTH), row),
            pl.BlockSpec((tm, SSD_WIDTH), row),
            pl.BlockSpec((tm, SSD_WIDTH), row),
            pl.BlockSpec((1, SSD_WIDTH), full),
            pl.BlockSpec((1, SSD_WIDTH), full),
            pl.BlockSpec((D_MODEL, D_MODEL), full),
        ],
        out_specs=pl.BlockSpec((tm, D_MODEL), row),
        out_shape=jax.ShapeDtypeStruct((n, D_MODEL), F32),
        compiler_params=_cparams(("parallel",)),
        name="outproj",
    )(x, att, xbc_act, yf, yb, z, d_x, snw, w_out)


PEER_TB = 1024
PEER_PIECE = 1024
PEER_EC = 1024
PEER_M = PEER_EC // PEER_N_KEYS
XU_PARTS = 4
BF16_ROWS = 16
N_JT = PEER_N_KEYS // BF16_ROWS


SUBLANES = 8
N_ROWV = PEER_N_KEYS // SUBLANES


def _oddeven_merge_sort_pairs(n):
    out, p = [], 1
    while p < n:
        k = p
        while k >= 1:
            for j in range(k % p, n - k, 2 * k):
                for i in range(min(k, n - j - k)):
                    if (i + j) // (2 * p) == (i + j + k) // (2 * p):
                        out.append((i + j, i + j + k))
            k //= 2
        p *= 2
    return out


SORT16_PAIRS = _oddeven_merge_sort_pairs(PEER_TOPK)


def _compare_exchange(rows, i, j):
    a, b = rows[i], rows[j]
    rows[i] = jnp.maximum(a, b)
    rows[j] = jnp.minimum(a, b)


def _sort_desc(rows):
    for i, j in SORT16_PAIRS:
        if j < len(rows):
            _compare_exchange(rows, i, j)


def _top16_over_sublanes(rows):
    rows = list(rows) + [None] * (PEER_TOPK - len(rows))
    for shift in (4, 2, 1):
        other = [None if r is None else pltpu.roll(r, shift, 0) for r in rows]
        merged = []
        for k in range(PEER_TOPK):
            a, b = rows[k], other[PEER_TOPK - 1 - k]
            merged.append(b if a is None else a if b is None else jnp.maximum(a, b))
        rows = merged
        for stride in (8, 4, 2, 1):
            for k in range(PEER_TOPK):
                if k & stride == 0:
                    _compare_exchange(rows, k, k + stride)
    return rows


def _sum_over_sublanes(x):
    for shift in (4, 2, 1):
        x = x + pltpu.roll(x, shift, 0)
    return x


def _peer_select_tile(s0_s, s1_s, lanes, lt, h, r1_s, e1_s, n_s, c_s):
    rowsl = lambda ref, k: ref[k * SUBLANES:(k + 1) * SUBLANES, lanes]
    s0 = [rowsl(s0_s, k) for k in range(N_ROWV)]
    s1 = [rowsl(s1_s, k) for k in range(N_ROWV)]
    v0, v1 = list(s0), list(s1)
    _sort_desc(v0)
    _sort_desc(v1)
    v0 = _top16_over_sublanes(v0)
    v1 = _top16_over_sublanes(v1)
    sub = lax.broadcasted_iota(jnp.int32, (SUBLANES, LANES), 0)

    def pack(vals):
        out = vals[0]
        for s in range(1, SUBLANES):
            out = jnp.where(sub == s, vals[s], out)
        return out

    v1_lo, v1_hi, v0_hi = pack(v1[0:8]), pack(v1[8:16]), pack(v0[8:16])
    cand = [v0[0] + v1_lo, v0[0] + v1_hi] + [v0[a] + v1_lo for a in range(1, 8)] + [v0_hi + v1[0]]
    csort = list(cand)
    _sort_desc(csort)
    csort = _top16_over_sublanes(csort)
    tau, cmax = csort[PEER_TOPK - 1], csort[0]
    zsum = jnp.zeros_like(tau)
    for cv in cand:
        zsum = zsum + jnp.where(cv >= tau, jnp.exp(cv - cmax), 0.0)
    rz = 1.0 / _sum_over_sublanes(zsum)
    hit = lambda x: jnp.where(x >= tau, 1.0, 0.0)
    n_a = []
    for a in range(PEER_TOPK):
        reach = PEER_TOPK // (a + 1)
        if reach > SUBLANES:
            n_a.append(_sum_over_sublanes(hit(v0[a] + v1_lo) + hit(v0[a] + v1_hi)))
        elif reach > 3:
            n_a.append(_sum_over_sublanes(hit(v0[a] + v1_lo)))
        else:
            n_a.append(sum(hit(v0[a] + v1[b]) for b in range(reach)))
    for m in range(N_JT):
        packed_r, packed_e = [], []
        for k in (2 * m, 2 * m + 1):
            n = jnp.zeros_like(tau)
            r = jnp.full_like(tau, float(PEER_TOPK))
            for a in range(PEER_TOPK):
                n = jnp.where(s0[k] == v0[a], n_a[a], n)
                r = jnp.where(s1[k] == v1[a], float(a), r)
            n_s[h, k, lt] = n
            c_s[h, k, lt] = jnp.exp(s0[k] - v0[0]) * rz
            packed_r.append(r)
            packed_e.append(jnp.exp(s1[k] - v1[0]))
        r1_s[h, m, :, lanes] = jnp.concatenate(packed_r, axis=0).astype(BF16)
        e1_s[h, m, :, lanes] = jnp.concatenate(packed_e, axis=0).astype(BF16)


def _peer_kernel(x1_ref, fw_ref, wqt_ref, sk_ref, u_ref, vt_ref, nw_ref, o_ref,
                 hf_s, qt_s, r1_s, e1_s, n_s, c_s, s0_s, s1_s, acc_s):
    c = pl.program_id(1)

    @pl.when(c == 0)
    def _prepare():
        x1 = x1_ref[...]
        ms = jnp.mean(x1 * x1, axis=-1, keepdims=True)
        hf_s[...] = (x1 * lax.rsqrt(ms + EPS) * fw_ref[...]).T.astype(BF16)
        qt_s[...] = jnp.dot(wqt_ref[...], hf_s[...], preferred_element_type=F32).astype(BF16)
        acc_s[...] = jnp.zeros_like(acc_s)

        def per_head(h, carry):
            r0 = pl.multiple_of(h * 2 * PEER_D_KEY, 2 * PEER_D_KEY)
            s0 = jnp.dot(sk_ref[2 * h], qt_s[pl.ds(r0, PEER_D_KEY), :], preferred_element_type=F32)
            s1 = jnp.dot(sk_ref[2 * h + 1], qt_s[pl.ds(r0 + PEER_D_KEY, PEER_D_KEY), :],
                         preferred_element_type=F32)
            s0_s[...] = s0
            s1_s[...] = s1

            def per_tile(lt, carry2):
                lanes = pl.ds(pl.multiple_of(lt * LANES, LANES), LANES)
                _peer_select_tile(s0_s, s1_s, lanes, lt, h, r1_s, e1_s, n_s, c_s)
                return carry2

            lax.fori_loop(0, PEER_TB // LANES, per_tile, 0)
            return carry

        lax.fori_loop(0, PEER_HEADS, per_head, 0)

    n_pieces = PEER_TB // PEER_PIECE

    def expert_scores(p, part):
        rows = slice(part * (PEER_EC // XU_PARTS), (part + 1) * (PEER_EC // XU_PARTS))
        return jnp.dot(u_ref[rows, :], hf_s[:, p * PEER_PIECE:(p + 1) * PEER_PIECE],
                       preferred_element_type=F32)

    for p in range(n_pieces):
        lanes = slice(p * PEER_PIECE, (p + 1) * PEER_PIECE)
        a_parts = []
        for part in range(XU_PARTS):
            xb = expert_scores(p, part).astype(BF16)
            act = (0.5 * xb) * (1.0 + lax.erf(xb * (2.0 ** -0.5)))
            gates = []
            for ii in range(part * PEER_M // XU_PARTS, (part + 1) * PEER_M // XU_PARTS):
                g = jnp.zeros((N_JT, BF16_ROWS, PEER_PIECE), BF16)
                for h in range(PEER_HEADS):
                    tiles = range(lanes.start // LANES, lanes.stop // LANES)
                    nrow = jnp.concatenate([n_s[h, c, t, pl.ds(ii, SUBLANES, stride=0), :] for t in tiles], axis=1)
                    crow = jnp.concatenate([c_s[h, c, t, pl.ds(ii, SUBLANES, stride=0), :] for t in tiles], axis=1)
                    nb = jnp.concatenate([nrow, nrow], axis=0).astype(BF16)
                    cb = jnp.concatenate([crow, crow], axis=0).astype(BF16)
                    sel = jnp.where(r1_s[h, :, :, lanes] < nb[None], e1_s[h, :, :, lanes], jnp.zeros((), BF16))
                    g = g + sel * cb[None]
                gates.append(g.reshape(PEER_N_KEYS, PEER_PIECE))
            a_parts.append(act * jnp.concatenate(gates, axis=0))
        a_t = jnp.concatenate(a_parts, axis=0)
        acc_s[:, lanes] += jnp.dot(vt_ref[...], a_t, preferred_element_type=F32)

    @pl.when(c == pl.num_programs(1) - 1)
    def _finish():
        y = x1_ref[...] + acc_s[...].T
        ms = jnp.mean(y * y, axis=-1, keepdims=True)
        o_ref[...] = y * lax.rsqrt(ms + EPS) * nw_ref[...]


def _peer(x1, fw, wq_t, sub_keys, exp_u, exp_vt, nw):
    n = x1.shape[0]
    tb, ec = PEER_TB, PEER_EC
    n_chunks = PEER_N_EXPERTS // ec
    tok = lambda t, c: (t, 0)
    const2 = lambda t, c: (0, 0)
    head_tiles = (PEER_HEADS, N_JT, BF16_ROWS, tb)
    key_tiles = (PEER_HEADS, N_ROWV, tb // LANES, SUBLANES, LANES)
    assert PEER_M == SUBLANES
    return pl.pallas_call(
        _peer_kernel,
        grid=(n // tb, n_chunks),
        in_specs=[
            pl.BlockSpec((tb, D_MODEL), tok, pipeline_mode=pl.Buffered(1)),
            pl.BlockSpec((1, D_MODEL), const2),
            pl.BlockSpec((2 * PEER_HEADS * PEER_D_KEY, D_MODEL), const2, pipeline_mode=pl.Buffered(1)),
            pl.BlockSpec((2 * PEER_HEADS, PEER_N_KEYS, PEER_D_KEY), lambda t, c: (0, 0, 0)),
            pl.BlockSpec((ec, D_MODEL), lambda t, c: (c, 0)),
            pl.BlockSpec((D_MODEL, ec), lambda t, c: (0, c)),
            pl.BlockSpec((1, D_MODEL), const2),
        ],
        out_specs=pl.BlockSpec((tb, D_MODEL), tok),
        out_shape=jax.ShapeDtypeStruct((n, D_MODEL), F32),
        scratch_shapes=[
            pltpu.VMEM((D_MODEL, tb), BF16),
            pltpu.VMEM((2 * PEER_HEADS * PEER_D_KEY, tb), BF16),
            pltpu.VMEM(head_tiles, BF16),
            pltpu.VMEM(head_tiles, BF16),
            pltpu.VMEM(key_tiles, F32),
            pltpu.VMEM(key_tiles, F32),
            pltpu.VMEM((PEER_N_KEYS, tb), F32),
            pltpu.VMEM((PEER_N_KEYS, tb), F32),
            pltpu.VMEM((D_MODEL, tb), F32),
        ],
        compiler_params=_cparams(("parallel", "arbitrary")),
        name="peer",
    )(x1, fw, wq_t, sub_keys, exp_u, exp_vt, nw)


def _lane_vec(two_by_heads):
    v = two_by_heads.astype(F32).reshape(1, 2 * SSD_HEADS)
    return jnp.pad(v, ((0, 0), (0, LANES - 2 * SSD_HEADS)))


def _trunk(x, p):
    b, l, d = x.shape
    n = b * l
    xf = x.reshape(n, d)
    q, k, v, z, xbc, dt = _inproj(xf, p["norm_mix_w"], p["w_main"], p["w_dt"])
    att = _attention(q.reshape(b, l, -1), k.reshape(b, l, -1), v.reshape(b, l, -1), p["att_bias"])
    xbc_act = _conv_silu(xbc.reshape(b, l, -1), p["conv_w"], p["conv_b"])
    yf, yb = _ssd_scan(xbc_act, dt.reshape(b, l, -1), p["dt_bias_vec"], p["a_log_vec"], p["tri"], p["expand"])
    x1 = _outproj(xf, att.reshape(n, -1), xbc_act.reshape(n, -1), yf.reshape(n, -1), yb.reshape(n, -1),
                  z, p["d_x"], p["ssd_norm_w"], p["w_out"])
    y = _peer(x1, p["norm_ffn_w"], p["wq_t"], p["sub_keys"], p["exp_u"], p["exp_vt"], p["norm_final_w"])
    return y.reshape(b, l, d)


def kernel(x_prompt, x_sample, norm_mix_w, w_in, rpb, conv_w, conv_b, dt_bias, a_log, d_skip, ssd_norm_w,
           w_out, norm_ffn_w, w_query, sub_keys, expert_u, expert_v, norm_final_w):
    assert w_in.shape[0] == 1, "single layer"
    w_in0 = w_in[0]
    w_dt = w_in0[:, MAIN_COLS:]
    w_dt = jnp.concatenate([w_dt, w_dt, jnp.zeros((D_MODEL, LANES - 2 * SSD_HEADS), w_dt.dtype)], axis=1)
    r = jnp.arange(Q)
    tril = (r[None, :] <= r[:, None])
    tri = jnp.stack([tril, tril.T]).astype(BF16)
    hid = jnp.arange(SSD_WIDTH) // SSD_HEAD_DIM
    lane = jnp.arange(LANES)
    expand = jnp.stack([lane[:, None] == hid[None, :],
                        lane[:, None] == hid[None, :] + SSD_HEADS]).astype(BF16)
    p = {
        "norm_mix_w": norm_mix_w[0].reshape(1, -1),
        "w_main": w_in0[:, :MAIN_COLS].astype(BF16),
        "w_dt": w_dt.astype(BF16),
        "att_bias": _attn_bias(rpb[0]),
        "conv_w": conv_w[0],
        "conv_b": conv_b[0].reshape(1, -1),
        "dt_bias_vec": _lane_vec(dt_bias[0]),
        "a_log_vec": _lane_vec(a_log[0]),
        "tri": tri,
        "expand": expand,
        "d_x": jnp.repeat(d_skip[0].astype(F32), SSD_HEAD_DIM).reshape(1, -1),
        "ssd_norm_w": ssd_norm_w[0].reshape(1, -1),
        "w_out": w_out[0].astype(BF16),
        "norm_ffn_w": norm_ffn_w[0].reshape(1, -1),
        "wq_t": w_query[0].T.astype(BF16),
        "sub_keys": sub_keys[0].reshape(2 * PEER_HEADS, PEER_N_KEYS, PEER_D_KEY).astype(BF16),
        "exp_u": expert_u[0].astype(BF16),
        "exp_vt": expert_v[0].T.astype(BF16),
        "norm_final_w": norm_final_w.reshape(1, -1),
    }
    return (_trunk(x_prompt, p), _trunk(x_sample, p))
```

```python
import functools
import math

import jax
import jax.numpy as jnp
import numpy as np
from jax import lax
from jax.experimental import pallas as pl
from jax.experimental.pallas import tpu as pltpu

F32 = jnp.float32
BF16 = jnp.bfloat16

D_MODEL = 1024
GRID_W = 64
ATT_HEADS = 8
ATT_HEAD_DIM = 64
ATT_WIDTH = 512
WIN_H = 8
WIN_W = 16
SSD_HEADS = 8
SSD_HEAD_DIM = 64
SSD_WIDTH = 512
SSD_GROUPS = 2
SSD_STATE = 128
SSD_CONV = 5
SSD_CHUNK = 128
CONV_CH = 1024
MAIN_COLS = 3 * ATT_WIDTH + SSD_WIDTH + CONV_CH
PEER_HEADS = 8
PEER_N_KEYS = 128
PEER_N_EXPERTS = PEER_N_KEYS * PEER_N_KEYS
PEER_D_KEY = 128
PEER_TOPK = 16
EPS = 1e-6

LANES = 128
VMEM_LIMIT = 60 * 1024 * 1024
NEG_BIG = -1e30
NT_DIMS = (((1,), (1,)), ((), ()))


def _cparams(sem):
    return pltpu.CompilerParams(dimension_semantics=sem, vmem_limit_bytes=VMEM_LIMIT)


def _split3(v):
    hi = v.astype(BF16)
    r1 = v - hi.astype(F32)
    mid = r1.astype(BF16)
    lo = (r1 - mid.astype(F32)).astype(BF16)
    return hi, mid, lo


IN_TM = 1024


def _inproj_kernel(x_ref, nw_ref, w_ref, wdt_ref, q_ref, k_ref, v_ref, z_ref, xbc_ref, dt_ref):
    x = x_ref[...]
    ms = jnp.mean(x * x, axis=-1, keepdims=True)
    h = (x * lax.rsqrt(ms + EPS) * nw_ref[...]).astype(BF16)

    def proj(lo, hi):
        return jnp.dot(h, w_ref[:, lo:hi], preferred_element_type=F32)

    q_ref[...] = proj(0, 512).astype(BF16)
    k_ref[...] = proj(512, 1024).astype(BF16)
    v_ref[...] = proj(1024, 1536).astype(BF16)
    z_ref[...] = proj(1536, 2048).astype(BF16)
    xbc_ref[...] = proj(2048, 3072)
    dt_ref[...] = jnp.dot(h, wdt_ref[...], preferred_element_type=F32)


def _inproj(x, nw, w_main, w_dt):
    n = x.shape[0]
    tm = IN_TM
    row = lambda i: (i, 0)
    full = lambda i: (0, 0)
    return pl.pallas_call(
        _inproj_kernel,
        grid=(n // tm,),
        in_specs=[
            pl.BlockSpec((tm, D_MODEL), row),
            pl.BlockSpec((1, D_MODEL), full),
            pl.BlockSpec((D_MODEL, MAIN_COLS), full),
            pl.BlockSpec((D_MODEL, LANES), full),
        ],
        out_specs=[
            pl.BlockSpec((tm, 512), row),
            pl.BlockSpec((tm, 512), row),
            pl.BlockSpec((tm, 512), row),
            pl.BlockSpec((tm, 512), row),
            pl.BlockSpec((tm, 1024), row),
            pl.BlockSpec((tm, LANES), row),
        ],
        out_shape=[
            jax.ShapeDtypeStruct((n, 512), BF16),
            jax.ShapeDtypeStruct((n, 512), BF16),
            jax.ShapeDtypeStruct((n, 512), BF16),
            jax.ShapeDtypeStruct((n, 512), BF16),
            jax.ShapeDtypeStruct((n, 1024), F32),
            jax.ShapeDtypeStruct((n, LANES), F32),
        ],
        compiler_params=_cparams(("parallel",)),
        name="inproj",
    )(x, nw, w_main, w_dt)


ATT_RB = 8
ATT_TOK = ATT_RB * GRID_W
ATT_HALO = WIN_H // 2
ATT_UROWS = ATT_RB + 2 * ATT_HALO
ATT_KEYS = ATT_UROWS * GRID_W
ATT_VARIANTS = 3
ATT_SUB = 4
ATT_PAIRS = 2
ATT_SUB_TOK = ATT_SUB * GRID_W
ATT_SUB_KEYS = (ATT_SUB + 2 * ATT_HALO) * GRID_W


def _attn_kernel(q_ref, kp_ref, kc_ref, kn_ref, vp_ref, vc_ref, vn_ref, bias_ref, o_ref):
    rb = pl.program_id(2)
    nrb = pl.num_programs(2)
    variant = jnp.where(rb == 0, 0, jnp.where(rb == nrb - 1, 2, 1))
    halo = ATT_HALO * GRID_W
    kw = jnp.concatenate([kp_ref[ATT_TOK - halo:, :], kc_ref[...], kn_ref[0:halo, :]], axis=0)
    vw = jnp.concatenate([vp_ref[ATT_TOK - halo:, :], vc_ref[...], vn_ref[0:halo, :]], axis=0)
    lane = lax.broadcasted_iota(jnp.int32, (ATT_SUB_TOK, LANES), 1)
    scale = jnp.asarray(ATT_HEAD_DIM ** -0.5, BF16)
    for pr in range(ATT_PAIRS):
        cols = slice(pr * LANES, (pr + 1) * LANES)
        for sb in range(ATT_RB // ATT_SUB):
            q = q_ref[sb * ATT_SUB_TOK:(sb + 1) * ATT_SUB_TOK, cols] * scale
            ks = kw[sb * ATT_SUB_TOK:sb * ATT_SUB_TOK + ATT_SUB_KEYS, cols]
            vs = vw[sb * ATT_SUB_TOK:sb * ATT_SUB_TOK + ATT_SUB_KEYS, cols]
            acc = jnp.zeros((ATT_SUB_TOK, LANES), F32)
            for hh in range(2):
                inh = (lane >= hh * ATT_HEAD_DIM) & (lane < (hh + 1) * ATT_HEAD_DIM)
                qm = jnp.where(inh, q, jnp.zeros_like(q))
                s = lax.dot_general(qm, ks, NT_DIMS, preferred_element_type=F32)
                s = s + bias_ref[2 * pr + hh, variant, sb]
                m = jnp.max(s, axis=-1, keepdims=True)
                p = jnp.exp(s - m)
                l = jnp.sum(p, axis=-1, keepdims=True)
                o = jnp.dot(p.astype(BF16), vs, preferred_element_type=F32) / l
                acc = jnp.where(inh, o, acc)
            o_ref[sb * ATT_SUB_TOK:(sb + 1) * ATT_SUB_TOK, cols] = acc.astype(BF16)


def _attn_bias(rpb):
    c = np.arange(GRID_W)
    col_start = np.clip(c - WIN_W // 2, 0, GRID_W - WIN_W)
    col_in = (c[None, :] >= col_start[:, None]) & (c[None, :] < col_start[:, None] + WIN_W)
    n_dr, n_dc = 2 * WIN_H - 1, 2 * WIN_W - 1
    left = GRID_W - 1 - (WIN_W - 1)
    ext = jnp.pad(rpb.astype(F32), ((0, 0), (0, 0), (left, 2 * GRID_W - n_dc - left)))
    flat = jnp.broadcast_to(ext[:, :, None, :], (ATT_HEADS, n_dr, GRID_W, 2 * GRID_W)).reshape(ATT_HEADS, n_dr, -1)
    skew = flat[:, :, GRID_W - 1:GRID_W - 1 + GRID_W * (2 * GRID_W - 1)].reshape(ATT_HEADS, n_dr, GRID_W, 2 * GRID_W - 1)
    t = jnp.where(col_in[None, None], skew[:, :, :, :GRID_W], NEG_BIG)
    t = jnp.transpose(t, (0, 2, 1, 3)).reshape(ATT_HEADS, GRID_W, (2 * WIN_H - 1) * GRID_W)
    sub_rows = ATT_SUB + 2 * ATT_HALO
    n_sub = ATT_RB // ATT_SUB
    base = jnp.stack([t[:, :, (ATT_SUB - 1 - r) * GRID_W:(ATT_SUB - 1 - r + sub_rows) * GRID_W]
                      for _ in range(n_sub) for r in range(ATT_SUB)], axis=1)
    base = base.reshape(ATT_HEADS, 1, n_sub, ATT_SUB_TOK, ATT_SUB_KEYS)
    qr = np.arange(ATT_RB)[:, None]
    u = (qr // ATT_SUB) * ATT_SUB + np.arange(sub_rows)[None, :]
    first = np.maximum(qr - ATT_HALO, 0) + ATT_HALO
    last = np.minimum(qr - ATT_HALO, 0) + ATT_HALO
    starts = np.stack([np.broadcast_to(s, u.shape) for s in (first, qr, last)])
    valid = (u[None] >= starts) & (u[None] < starts + WIN_H)
    valid = np.repeat(np.repeat(valid, GRID_W, axis=1), GRID_W, axis=2)
    valid = valid.reshape(1, ATT_VARIANTS, n_sub, ATT_SUB_TOK, ATT_SUB_KEYS)
    return jnp.where(valid, base, NEG_BIG)


def _attention(q, k, v, bias):
    b, l, _ = q.shape
    rows = l // GRID_W
    assert rows % ATT_RB == 0 and rows // ATT_RB >= 2
    nrb = rows // ATT_RB
    cur = lambda hp, bi, rb: (bi, rb, hp)
    prv = lambda hp, bi, rb: (bi, jnp.maximum(rb - 1, 0), hp)
    nxt = lambda hp, bi, rb: (bi, jnp.minimum(rb + 1, nrb - 1), hp)
    blk = (None, ATT_TOK, ATT_PAIRS * LANES)
    return pl.pallas_call(
        _attn_kernel,
        grid=(ATT_HEADS // (2 * ATT_PAIRS), b, nrb),
        in_specs=[
            pl.BlockSpec(blk, cur),
            pl.BlockSpec(blk, prv), pl.BlockSpec(blk, cur), pl.BlockSpec(blk, nxt),
            pl.BlockSpec(blk, prv), pl.BlockSpec(blk, cur), pl.BlockSpec(blk, nxt),
            pl.BlockSpec((2 * ATT_PAIRS, ATT_VARIANTS, ATT_RB // ATT_SUB, ATT_SUB_TOK, ATT_SUB_KEYS),
                         lambda hp, bi, rb: (hp, 0, 0, 0, 0),
                         pipeline_mode=pl.Buffered(1)),
        ],
        out_specs=pl.BlockSpec(blk, cur),
        out_shape=jax.ShapeDtypeStruct((b, l, ATT_WIDTH), BF16),
        compiler_params=_cparams(("parallel", "parallel", "parallel")),
        name="nbr_attention",
    )(q, k, k, k, v, v, v, bias)


CONV_TL = 512
HALO = 8


def _conv_kernel(xp_ref, xc_ref, xn_ref, w_ref, b_ref, o_ref, buf):
    i = pl.program_id(1)
    nblk = pl.num_programs(1)
    prev = xp_ref[...]
    nxt = xn_ref[...]
    buf[0:HALO] = jnp.where(i == 0, jnp.zeros_like(prev), prev)
    buf[HALO:HALO + CONV_TL] = xc_ref[...]
    buf[HALO + CONV_TL:2 * HALO + CONV_TL] = jnp.where(i == nblk - 1, jnp.zeros_like(nxt), nxt)
    x = buf[...]
    rows = CONV_TL + 2 * HALO
    w = [w_ref[j:j + 1, :] for j in range(SSD_CONV)]
    up = lambda a: pltpu.roll(a, rows - 1, 0)
    down = lambda a: pltpu.roll(a, 1, 0)
    y = x * w[2] + up(up(x * w[4]) + x * w[3]) + down(down(x * w[0]) + x * w[1])
    y = y[HALO:HALO + CONV_TL, :] + b_ref[...]
    o_ref[...] = y * (1.0 / (1.0 + jnp.exp(-y)))


def _conv_silu(xbc, conv_w, conv_b):
    b, l, ch = xbc.shape
    tl = CONV_TL
    nblk = l // tl
    per = tl // HALO
    nh = l // HALO
    return pl.pallas_call(
        _conv_kernel,
        grid=(b, nblk),
        in_specs=[
            pl.BlockSpec((None, HALO, ch), lambda bi, i: (bi, jnp.maximum(i * per - 1, 0), 0)),
            pl.BlockSpec((None, tl, ch), lambda bi, i: (bi, i, 0)),
            pl.BlockSpec((None, HALO, ch), lambda bi, i: (bi, jnp.minimum((i + 1) * per, nh - 1), 0)),
            pl.BlockSpec((SSD_CONV, ch), lambda bi, i: (0, 0)),
            pl.BlockSpec((1, ch), lambda bi, i: (0, 0)),
        ],
        out_specs=pl.BlockSpec((None, tl, ch), lambda bi, i: (bi, i, 0)),
        out_shape=jax.ShapeDtypeStruct((b, l, ch), F32),
        scratch_shapes=[pltpu.VMEM((tl + 2 * HALO, ch), F32)],
        compiler_params=_cparams(("parallel", "parallel")),
        name="conv_silu",
    )(xbc, xbc, xbc, conv_w, conv_b)


Q = SSD_CHUNK
SSD_CPS = 4


def _ssd_direction(xs_ref, bm_ref, cm_ref, dt_ref, bias, a_vec, tri, expand_m, y_ref, s_ref, fwd, rows):
    xx = dt_ref[rows, :] + bias
    dtv = jnp.maximum(xx, 0.0) + jnp.log1p(jnp.exp(-jnp.abs(xx)))
    adt = dtv * a_vec
    hi, mid, lo = _split3(adt)
    cs3 = jnp.dot(tri, jnp.concatenate([hi, mid, lo], axis=1), preferred_element_type=F32)
    cs = cs3[:, 0:LANES] + cs3[:, LANES:2 * LANES] + cs3[:, 2 * LANES:3 * LANES]

    def expand(v):
        h3, m3, _ = _split3(v)
        r = jnp.dot(jnp.concatenate([h3, m3], axis=0), expand_m, preferred_element_type=F32)
        return r[0:Q] + r[Q:2 * Q]

    dt_x = expand(dtv)
    cs_x = expand(cs)
    last = Q - 1 if fwd else 0
    tot_x = cs_x[last:last + 1, :]
    xdt = xs_ref[rows, :] * dt_x
    xdt_b = xdt.astype(BF16)
    xw = (xdt * jnp.exp(tot_x - cs_x)).astype(BF16)
    scale_off = jnp.exp(cs_x)
    chunk_decay = jnp.exp(tot_x)
    cs_t = cs.T
    row = lax.broadcasted_iota(jnp.int32, (Q, Q), 0)
    col = lax.broadcasted_iota(jnp.int32, (Q, Q), 1)
    tri_mask = (col <= row) if fwd else (col >= row)
    gw = SSD_WIDTH // SSD_GROUPS
    for g in range(SSD_GROUPS):
        bg = bm_ref[rows, g * SSD_STATE:(g + 1) * SSD_STATE]
        cg = cm_ref[rows, g * SSD_STATE:(g + 1) * SSD_STATE].astype(BF16)
        cb = lax.dot_general(cg, bg.astype(BF16), NT_DIMS, preferred_element_type=F32)
        state = s_ref[g]
        yoff = jnp.dot(cg, state.astype(BF16), preferred_element_type=F32) * scale_off[:, g * gw:(g + 1) * gw]
        bg_t = bg.T.astype(BF16)
        s_ref[g] = state * chunk_decay[:, g * gw:(g + 1) * gw] + jnp.dot(
            bg_t, xw[:, g * gw:(g + 1) * gw], preferred_element_type=F32)
        for pp in range(2):
            pair = 2 * g + pp
            xpair = xdt_b[:, pair * LANES:(pair + 1) * LANES]
            ypair = yoff[:, pp * LANES:(pp + 1) * LANES]
            for hh in range(2):
                hc = 2 * pair + hh + (0 if fwd else SSD_HEADS)
                diff = cs[:, hc:hc + 1] - cs_t[hc:hc + 1, :]
                lm = jnp.where(tri_mask, jnp.exp(jnp.minimum(diff, 0.0)), 0.0)
                mm = (cb * lm).astype(BF16)
                inh = (col >= hh * SSD_HEAD_DIM) & (col < (hh + 1) * SSD_HEAD_DIM)
                ypair = ypair + jnp.dot(mm, jnp.where(inh, xpair, jnp.zeros_like(xpair)),
                                        preferred_element_type=F32)
            y_ref[rows, pair * LANES:(pair + 1) * LANES] = ypair.astype(y_ref.dtype)


def _ssd_kernel(xs_f, bm_f, cm_f, dt_f, xs_b, bm_b, cm_b, dt_b, bias_ref, alog_ref, tri_ref, e_ref,
                yf_ref, yb_ref, sf_ref, sb_ref):
    @pl.when(pl.program_id(1) == 0)
    def _():
        sf_ref[...] = jnp.zeros_like(sf_ref)
        sb_ref[...] = jnp.zeros_like(sb_ref)

    lane = lax.broadcasted_iota(jnp.int32, (1, LANES), 1)
    bias = bias_ref[...]
    a_vec = jnp.where(lane < 2 * SSD_HEADS, -jnp.exp(alog_ref[...]), 0.0)
    for k in range(SSD_CPS):
        up = slice(k * Q, (k + 1) * Q)
        down = slice((SSD_CPS - 1 - k) * Q, (SSD_CPS - k) * Q)
        _ssd_direction(xs_f, bm_f, cm_f, dt_f, bias, a_vec, tri_ref[0], e_ref[0], yf_ref, sf_ref, True, up)
        _ssd_direction(xs_b, bm_b, cm_b, dt_b, bias, a_vec, tri_ref[1], e_ref[1], yb_ref, sb_ref, False, down)


def _ssd_scan(xbc_act, dt, bias_vec, alog_vec, tri, expand_m):
    b, l, _ = xbc_act.shape
    nc = l // (Q * SSD_CPS)
    blk = Q * SSD_CPS
    fw = lambda cb: (lambda bi, c: (bi, c, cb))
    bw = lambda cb: (lambda bi, c: (bi, nc - 1 - c, cb))
    const2 = lambda bi, c: (0, 0)
    const3 = lambda bi, c: (0, 0, 0)

    def chunk_specs(mk):
        return [
            pl.BlockSpec((None, blk, SSD_WIDTH), mk(0)),
            pl.BlockSpec((None, blk, 2 * SSD_STATE), mk(2)),
            pl.BlockSpec((None, blk, 2 * SSD_STATE), mk(3)),
            pl.BlockSpec((None, blk, LANES), mk(0)),
        ]

    return pl.pallas_call(
        _ssd_kernel,
        grid=(b, nc),
        in_specs=chunk_specs(fw) + chunk_specs(bw) + [
            pl.BlockSpec((1, LANES), const2),
            pl.BlockSpec((1, LANES), const2),
            pl.BlockSpec((2, Q, Q), const3),
            pl.BlockSpec((2, LANES, SSD_WIDTH), const3),
        ],
        out_specs=[
            pl.BlockSpec((None, blk, SSD_WIDTH), fw(0)),
            pl.BlockSpec((None, blk, SSD_WIDTH), bw(0)),
        ],
        out_shape=[jax.ShapeDtypeStruct((b, l, SSD_WIDTH), BF16)] * 2,
        scratch_shapes=[pltpu.VMEM((SSD_GROUPS, SSD_STATE, 256), F32),
                        pltpu.VMEM((SSD_GROUPS, SSD_STATE, 256), F32)],
        compiler_params=_cparams(("parallel", "arbitrary")),
        name="ssd_scan",
    )(xbc_act, xbc_act, xbc_act, dt, xbc_act, xbc_act, xbc_act, dt, bias_vec, alog_vec, tri, expand_m)


OUT_TM = 1024


def _outproj_kernel(x_ref, att_ref, xs_ref, yf_ref, yb_ref, z_ref, dx_ref, snw_ref, wo_ref, x1_ref):
    z = z_ref[...].astype(F32)
    y = xs_ref[...] * dx_ref[...] + yf_ref[...].astype(F32) + yb_ref[...].astype(F32)
    y = y * (z * (1.0 / (1.0 + jnp.exp(-z))))
    ms = jnp.mean(y * y, axis=-1, keepdims=True)
    ssd = (y * lax.rsqrt(ms + EPS) * snw_ref[...]).astype(BF16)
    mixed = jnp.dot(att_ref[...], wo_ref[0:ATT_WIDTH, :], preferred_element_type=F32)
    mixed = mixed + jnp.dot(ssd, wo_ref[ATT_WIDTH:, :], preferred_element_type=F32)
    x1_ref[...] = x_ref[...] + mixed


def _outproj(x, att, xbc_act, yf, yb, z, d_x, snw, w_out):
    n = x.shape[0]
    tm = OUT_TM
    row = lambda i: (i, 0)
    full = lambda i: (0, 0)
    return pl.pallas_call(
        _outproj_kernel,
        grid=(n // tm,),
        in_specs=[
            pl.BlockSpec((tm, D_MODEL), row),
            pl.BlockSpec((tm, ATT_WIDTH), row),
            pl.BlockSpec((tm, SSD_WIDTH), row),
            pl.BlockSpec((tm, SSD_WIDTH), row),
            pl.BlockSpec((tm, SSD_WIDTH), row),
            pl.BlockSpec((tm, SSD_WIDTH), row),
            pl.BlockSpec((1, SSD_WIDTH), full),
            pl.BlockSpec((1, SSD_WIDTH), full),
            pl.BlockSpec((D_MODEL, D_MODEL), full),
        ],
        out_specs=pl.BlockSpec((tm, D_MODEL), row),
        out_shape=jax.ShapeDtypeStruct((n, D_MODEL), F32),
        compiler_params=_cparams(("parallel",)),
        name="outproj",
    )(x, att, xbc_act, yf, yb, z, d_x, snw, w_out)


PEER_TB = 1024
PEER_PIECE = 1024
PEER_EC = 1024
PEER_M = PEER_EC // PEER_N_KEYS
XU_PARTS = 4
BF16_ROWS = 16
N_JT = PEER_N_KEYS // BF16_ROWS


SUBLANES = 8
N_ROWV = PEER_N_KEYS // SUBLANES


def _oddeven_merge_sort_pairs(n):
    out, p = [], 1
    while p < n:
        k = p
        while k >= 1:
            for j in range(k % p, n - k, 2 * k):
                for i in range(min(k, n - j - k)):
                    if (i + j) // (2 * p) == (i + j + k) // (2 * p):
                        out.append((i + j, i + j + k))
            k //= 2
        p *= 2
    return out


SORT16_PAIRS = _oddeven_merge_sort_pairs(PEER_TOPK)


def _compare_exchange(rows, i, j):
    a, b = rows[i], rows[j]
    rows[i] = jnp.maximum(a, b)
    rows[j] = jnp.minimum(a, b)


def _sort_desc(rows):
    for i, j in SORT16_PAIRS:
        if j < len(rows):
            _compare_exchange(rows, i, j)


def _top16_over_sublanes(rows, sort_last=True):
    rows = list(rows) + [None] * (PEER_TOPK - len(rows))
    for shift in (4, 2, 1):
        other = [None if r is None else pltpu.roll(r, shift, 0) for r in rows]
        merged = []
        for k in range(PEER_TOPK):
            a, b = rows[k], other[PEER_TOPK - 1 - k]
            merged.append(b if a is None else a if b is None else jnp.maximum(a, b))
        rows = merged
        if shift == 1 and not sort_last:
            break
        for stride in (8, 4, 2, 1):
            for k in range(PEER_TOPK):
                if k & stride == 0:
                    _compare_exchange(rows, k, k + stride)
    return rows


def _sum_over_sublanes(x):
    for shift in (4, 2, 1):
        x = x + pltpu.roll(x, shift, 0)
    return x


def _peer_select_tile(s0_s, s1_s, lanes, lt, h, r1_s, e1_s, n_s, c_s):
    rowsl = lambda ref, k: ref[k * SUBLANES:(k + 1) * SUBLANES, lanes]
    s0 = [rowsl(s0_s, k) for k in range(N_ROWV)]
    s1 = [rowsl(s1_s, k) for k in range(N_ROWV)]
    v0, v1 = list(s0), list(s1)
    _sort_desc(v0)
    _sort_desc(v1)
    v0 = _top16_over_sublanes(v0)
    v1 = _top16_over_sublanes(v1)
    sub = lax.broadcasted_iota(jnp.int32, (SUBLANES, LANES), 0)

    def pack(vals):
        out = vals[0]
        for s in range(1, SUBLANES):
            out = jnp.where(sub == s, vals[s], out)
        return out

    v1_lo, v1_hi, v0_hi = pack(v1[0:8]), pack(v1[8:16]), pack(v0[8:16])
    cand = [v0[0] + v1_lo, v0[0] + v1_hi] + [v0[a] + v1_lo for a in range(1, 8)] + [v0_hi + v1[0]]
    csort = list(cand)
    _sort_desc(csort)
    best16 = _top16_over_sublanes(csort, sort_last=False)
    tau = functools.reduce(jnp.minimum, best16)
    cmax = v0[0] + v1[0]
    zsum = jnp.zeros_like(tau)
    for cv in cand:
        zsum = zsum + jnp.where(cv >= tau, jnp.exp(cv - cmax), 0.0)
    rz = 1.0 / _sum_over_sublanes(zsum)
    hit = lambda x: jnp.where(x >= tau, 1.0, 0.0)
    n_a = []
    for a in range(PEER_TOPK):
        reach = PEER_TOPK // (a + 1)
        if reach > SUBLANES:
            n_a.append(_sum_over_sublanes(hit(v0[a] + v1_lo) + hit(v0[a] + v1_hi)))
        elif reach > 3:
            n_a.append(_sum_over_sublanes(hit(v0[a] + v1_lo)))
        else:
            n_a.append(sum(hit(v0[a] + v1[b]) for b in range(reach)))
    for m in range(N_JT):
        packed_r, packed_e = [], []
        for k in (2 * m, 2 * m + 1):
            n = jnp.zeros_like(tau)
            r = jnp.full_like(tau, float(PEER_TOPK))
            for a in range(PEER_TOPK):
                n = jnp.where(s0[k] == v0[a], n_a[a], n)
                r = jnp.where(s1[k] == v1[a], float(a), r)
            n_s[h, k, lt] = n
            c_s[h, k, lt] = jnp.exp(s0[k] - v0[0]) * rz
            packed_r.append(r)
            packed_e.append(jnp.exp(s1[k] - v1[0]))
        r1_s[h, m, :, lanes] = jnp.concatenate(packed_r, axis=0).astype(BF16)
        e1_s[h, m, :, lanes] = jnp.concatenate(packed_e, axis=0).astype(BF16)


def _peer_kernel(x1_ref, fw_ref, wqt_ref, sk_ref, u_ref, vt_ref, nw_ref, o_ref,
                 hf_s, qt_s, r1_s, e1_s, n_s, c_s, s0_s, s1_s, acc_s):
    c = pl.program_id(1)

    @pl.when(c == 0)
    def _prepare():
        x1 = x1_ref[...]
        ms = jnp.mean(x1 * x1, axis=-1, keepdims=True)
        hf_s[...] = (x1 * lax.rsqrt(ms + EPS) * fw_ref[...]).T.astype(BF16)
        qt_s[...] = jnp.dot(wqt_ref[...], hf_s[...], preferred_element_type=F32).astype(BF16)
        acc_s[...] = jnp.zeros_like(acc_s)

        def per_head(h, carry):
            r0 = pl.multiple_of(h * 2 * PEER_D_KEY, 2 * PEER_D_KEY)
            s0 = jnp.dot(sk_ref[2 * h], qt_s[pl.ds(r0, PEER_D_KEY), :], preferred_element_type=F32)
            s1 = jnp.dot(sk_ref[2 * h + 1], qt_s[pl.ds(r0 + PEER_D_KEY, PEER_D_KEY), :],
                         preferred_element_type=F32)
            s0_s[...] = s0
            s1_s[...] = s1

            def per_tile(lt, carry2):
                lanes = pl.ds(pl.multiple_of(lt * LANES, LANES), LANES)
                _peer_select_tile(s0_s, s1_s, lanes, lt, h, r1_s, e1_s, n_s, c_s)
                return carry2

            lax.fori_loop(0, PEER_TB // LANES, per_tile, 0)
            return carry

        lax.fori_loop(0, PEER_HEADS, per_head, 0)

    n_pieces = PEER_TB // PEER_PIECE

    def expert_scores(p, part):
        rows = slice(part * (PEER_EC // XU_PARTS), (part + 1) * (PEER_EC // XU_PARTS))
        return jnp.dot(u_ref[rows, :], hf_s[:, p * PEER_PIECE:(p + 1) * PEER_PIECE],
                       preferred_element_type=F32)

    for p in range(n_pieces):
        lanes = slice(p * PEER_PIECE, (p + 1) * PEER_PIECE)
        a_parts = []
        for part in range(XU_PARTS):
            xb = expert_scores(p, part).astype(BF16)
            act = (0.5 * xb) * (1.0 + lax.erf(xb * (2.0 ** -0.5)))
            gates = []
            for ii in range(part * PEER_M // XU_PARTS, (part + 1) * PEER_M // XU_PARTS):
                g = jnp.zeros((N_JT, BF16_ROWS, PEER_PIECE), BF16)
                for h in range(PEER_HEADS):
                    tiles = range(lanes.start // LANES, lanes.stop // LANES)
                    nrow = jnp.concatenate([n_s[h, c, t, pl.ds(ii, SUBLANES, stride=0), :] for t in tiles], axis=1)
                    crow = jnp.concatenate([c_s[h, c, t, pl.ds(ii, SUBLANES, stride=0), :] for t in tiles], axis=1)
                    nb = jnp.concatenate([nrow, nrow], axis=0).astype(BF16)
                    cb = jnp.concatenate([crow, crow], axis=0).astype(BF16)
                    sel = jnp.where(r1_s[h, :, :, lanes] < nb[None], e1_s[h, :, :, lanes], jnp.zeros((), BF16))
                    g = g + sel * cb[None]
                gates.append(g.reshape(PEER_N_KEYS, PEER_PIECE))
            a_parts.append(act * jnp.concatenate(gates, axis=0))
        a_t = jnp.concatenate(a_parts, axis=0)
        acc_s[:, lanes] += jnp.dot(vt_ref[...], a_t, preferred_element_type=F32)

    @pl.when(c == pl.num_programs(1) - 1)
    def _finish():
        y = x1_ref[...] + acc_s[...].T
        ms = jnp.mean(y * y, axis=-1, keepdims=True)
        o_ref[...] = y * lax.rsqrt(ms + EPS) * nw_ref[...]


def _peer(x1, fw, wq_t, sub_keys, exp_u, exp_vt, nw):
    n = x1.shape[0]
    tb, ec = PEER_TB, PEER_EC
    n_chunks = PEER_N_EXPERTS // ec
    tok = lambda t, c: (t, 0)
    const2 = lambda t, c: (0, 0)
    head_tiles = (PEER_HEADS, N_JT, BF16_ROWS, tb)
    key_tiles = (PEER_HEADS, N_ROWV, tb // LANES, SUBLANES, LANES)
    assert PEER_M == SUBLANES
    return pl.pallas_call(
        _peer_kernel,
        grid=(n // tb, n_chunks),
        in_specs=[
            pl.BlockSpec((tb, D_MODEL), tok, pipeline_mode=pl.Buffered(1)),
            pl.BlockSpec((1, D_MODEL), const2),
            pl.BlockSpec((2 * PEER_HEADS * PEER_D_KEY, D_MODEL), const2, pipeline_mode=pl.Buffered(1)),
            pl.BlockSpec((2 * PEER_HEADS, PEER_N_KEYS, PEER_D_KEY), lambda t, c: (0, 0, 0)),
            pl.BlockSpec((ec, D_MODEL), lambda t, c: (c, 0)),
            pl.BlockSpec((D_MODEL, ec), lambda t, c: (0, c)),
            pl.BlockSpec((1, D_MODEL), const2),
        ],
        out_specs=pl.BlockSpec((tb, D_MODEL), tok),
        out_shape=jax.ShapeDtypeStruct((n, D_MODEL), F32),
        scratch_shapes=[
            pltpu.VMEM((D_MODEL, tb), BF16),
            pltpu.VMEM((2 * PEER_HEADS * PEER_D_KEY, tb), BF16),
            pltpu.VMEM(head_tiles, BF16),
            pltpu.VMEM(head_tiles, BF16),
            pltpu.VMEM(key_tiles, F32),
            pltpu.VMEM(key_tiles, F32),
            pltpu.VMEM((PEER_N_KEYS, tb), F32),
            pltpu.VMEM((PEER_N_KEYS, tb), F32),
            pltpu.VMEM((D_MODEL, tb), F32),
        ],
        compiler_params=_cparams(("parallel", "arbitrary")),
        name="peer",
    )(x1, fw, wq_t, sub_keys, exp_u, exp_vt, nw)


def _lane_vec(two_by_heads):
    v = two_by_heads.astype(F32).reshape(1, 2 * SSD_HEADS)
    return jnp.pad(v, ((0, 0), (0, LANES - 2 * SSD_HEADS)))


def _trunk(x, p):
    b, l, d = x.shape
    n = b * l
    xf = x.reshape(n, d)
    q, k, v, z, xbc, dt = _inproj(xf, p["norm_mix_w"], p["w_main"], p["w_dt"])
    att = _attention(q.reshape(b, l, -1), k.reshape(b, l, -1), v.reshape(b, l, -1), p["att_bias"])
    xbc_act = _conv_silu(xbc.reshape(b, l, -1), p["conv_w"], p["conv_b"])
    yf, yb = _ssd_scan(xbc_act, dt.reshape(b, l, -1), p["dt_bias_vec"], p["a_log_vec"], p["tri"], p["expand"])
    x1 = _outproj(xf, att.reshape(n, -1), xbc_act.reshape(n, -1), yf.reshape(n, -1), yb.reshape(n, -1),
                  z, p["d_x"], p["ssd_norm_w"], p["w_out"])
    y = _peer(x1, p["norm_ffn_w"], p["wq_t"], p["sub_keys"], p["exp_u"], p["exp_vt"], p["norm_final_w"])
    return y.reshape(b, l, d)


def kernel(x_prompt, x_sample, norm_mix_w, w_in, rpb, conv_w, conv_b, dt_bias, a_log, d_skip, ssd_norm_w,
           w_out, norm_ffn_w, w_query, sub_keys, expert_u, expert_v, norm_final_w):
    assert w_in.shape[0] == 1, "single layer"
    w_in0 = w_in[0]
    w_dt = w_in0[:, MAIN_COLS:]
    w_dt = jnp.concatenate([w_dt, w_dt, jnp.zeros((D_MODEL, LANES - 2 * SSD_HEADS), w_dt.dtype)], axis=1)
    r = jnp.arange(Q)
    tril = (r[None, :] <= r[:, None])
    tri = jnp.stack([tril, tril.T]).astype(BF16)
    hid = jnp.arange(SSD_WIDTH) // SSD_HEAD_DIM
    lane = jnp.arange(LANES)
    expand = jnp.stack([lane[:, None] == hid[None, :],
                        lane[:, None] == hid[None, :] + SSD_HEADS]).astype(BF16)
    p = {
        "norm_mix_w": norm_mix_w[0].reshape(1, -1),
        "w_main": w_in0[:, :MAIN_COLS].astype(BF16),
        "w_dt": w_dt.astype(BF16),
        "att_bias": _attn_bias(rpb[0]),
        "conv_w": conv_w[0],
        "conv_b": conv_b[0].reshape(1, -1),
        "dt_bias_vec": _lane_vec(dt_bias[0]),
        "a_log_vec": _lane_vec(a_log[0]),
        "tri": tri,
        "expand": expand,
        "d_x": jnp.repeat(d_skip[0].astype(F32), SSD_HEAD_DIM).reshape(1, -1),
        "ssd_norm_w": ssd_norm_w[0].reshape(1, -1),
        "w_out": w_out[0].astype(BF16),
        "norm_ffn_w": norm_ffn_w[0].reshape(1, -1),
        "wq_t": w_query[0].T.astype(BF16),
        "sub_keys": sub_keys[0].reshape(2 * PEER_HEADS, PEER_N_KEYS, PEER_D_KEY).astype(BF16),
        "exp_u": expert_u[0].astype(BF16),
        "exp_vt": expert_v[0].T.astype(BF16),
        "norm_final_w": norm_final_w.reshape(1, -1),
    }
    return (_trunk(x_prompt, p), _trunk(x_sample, p))
```

```python
import functools
import math

import jax
import jax.numpy as jnp
import numpy as np
from jax import lax
from jax.experimental import pallas as pl
from jax.experimental.pallas import tpu as pltpu

F32 = jnp.float32
BF16 = jnp.bfloat16

D_MODEL = 1024
GRID_W = 64
ATT_HEADS = 8
ATT_HEAD_DIM = 64
ATT_WIDTH = 512
WIN_H = 8
WIN_W = 16
SSD_HEADS = 8
SSD_HEAD_DIM = 64
SSD_WIDTH = 512
SSD_GROUPS = 2
SSD_STATE = 128
SSD_CONV = 5
SSD_CHUNK = 128
CONV_CH = 1024
MAIN_COLS = 3 * ATT_WIDTH + SSD_WIDTH + CONV_CH
PEER_HEADS = 8
PEER_N_KEYS = 128
PEER_N_EXPERTS = PEER_N_KEYS * PEER_N_KEYS
PEER_D_KEY = 128
PEER_TOPK = 16
EPS = 1e-6

LANES = 128
VMEM_LIMIT = 60 * 1024 * 1024
NEG_BIG = -1e30
NT_DIMS = (((1,), (1,)), ((), ()))


def _cparams(sem):
    return pltpu.CompilerParams(dimension_semantics=sem, vmem_limit_bytes=VMEM_LIMIT)


def _split3(v):
    hi = v.astype(BF16)
    r1 = v - hi.astype(F32)
    mid = r1.astype(BF16)
    lo = (r1 - mid.astype(F32)).astype(BF16)
    return hi, mid, lo


IN_TM = 1024


def _inproj_kernel(x_ref, nw_ref, w_ref, wdt_ref, q_ref, k_ref, v_ref, z_ref, xbc_ref, dt_ref):
    x = x_ref[...]
    ms = jnp.mean(x * x, axis=-1, keepdims=True)
    h = (x * lax.rsqrt(ms + EPS) * nw_ref[...]).astype(BF16)

    def proj(lo, hi):
        return jnp.dot(h, w_ref[:, lo:hi], preferred_element_type=F32)

    q_ref[...] = proj(0, 512).astype(BF16)
    k_ref[...] = proj(512, 1024).astype(BF16)
    v_ref[...] = proj(1024, 1536).astype(BF16)
    z_ref[...] = proj(1536, 2048).astype(BF16)
    xbc_ref[...] = proj(2048, 3072)
    dt_ref[...] = jnp.dot(h, wdt_ref[...], preferred_element_type=F32)


def _inproj(x, nw, w_main, w_dt):
    n = x.shape[0]
    tm = IN_TM
    row = lambda i: (i, 0)
    full = lambda i: (0, 0)
    return pl.pallas_call(
        _inproj_kernel,
        grid=(n // tm,),
        in_specs=[
            pl.BlockSpec((tm, D_MODEL), row),
            pl.BlockSpec((1, D_MODEL), full),
            pl.BlockSpec((D_MODEL, MAIN_COLS), full),
            pl.BlockSpec((D_MODEL, LANES), full),
        ],
        out_specs=[
            pl.BlockSpec((tm, 512), row),
            pl.BlockSpec((tm, 512), row),
            pl.BlockSpec((tm, 512), row),
            pl.BlockSpec((tm, 512), row),
            pl.BlockSpec((tm, 1024), row),
            pl.BlockSpec((tm, LANES), row),
        ],
        out_shape=[
            jax.ShapeDtypeStruct((n, 512), BF16),
            jax.ShapeDtypeStruct((n, 512), BF16),
            jax.ShapeDtypeStruct((n, 512), BF16),
            jax.ShapeDtypeStruct((n, 512), BF16),
            jax.ShapeDtypeStruct((n, 1024), F32),
            jax.ShapeDtypeStruct((n, LANES), F32),
        ],
        compiler_params=_cparams(("parallel",)),
        name="inproj",
    )(x, nw, w_main, w_dt)


ATT_RB = 8
ATT_TOK = ATT_RB * GRID_W
ATT_HALO = WIN_H // 2
ATT_UROWS = ATT_RB + 2 * ATT_HALO
ATT_KEYS = ATT_UROWS * GRID_W
ATT_VARIANTS = 3
ATT_SUB = 2
ATT_SUB_START = (0, 2, 4, 4)
ATT_PAIRS = 2
ATT_SUB_TOK = ATT_SUB * GRID_W
ATT_SUB_ROWS = 12
ATT_SUB_KEYS = ATT_SUB_ROWS * GRID_W


def _attn_kernel(q_ref, kp_ref, kc_ref, kn_ref, vp_ref, vc_ref, vn_ref, bias_ref, o_ref):
    rb = pl.program_id(2)
    nrb = pl.num_programs(2)
    variant = jnp.where(rb == 0, 0, jnp.where(rb == nrb - 1, 2, 1))
    halo = ATT_HALO * GRID_W
    kw = jnp.concatenate([kp_ref[ATT_TOK - halo:, :], kc_ref[...], kn_ref[0:halo, :]], axis=0)
    vw = jnp.concatenate([vp_ref[ATT_TOK - halo:, :], vc_ref[...], vn_ref[0:halo, :]], axis=0)
    lane = lax.broadcasted_iota(jnp.int32, (ATT_SUB_TOK, LANES), 1)
    scale = jnp.asarray(ATT_HEAD_DIM ** -0.5, BF16)
    for pr in range(ATT_PAIRS):
        cols = slice(pr * LANES, (pr + 1) * LANES)
        for sb in range(ATT_RB // ATT_SUB):
            q = q_ref[sb * ATT_SUB_TOK:(sb + 1) * ATT_SUB_TOK, cols] * scale
            k0 = ATT_SUB_START[sb] * GRID_W
            ks = kw[k0:k0 + ATT_SUB_KEYS, cols]
            vs = vw[k0:k0 + ATT_SUB_KEYS, cols]
            acc = jnp.zeros((ATT_SUB_TOK, LANES), F32)
            for hh in range(2):
                inh = (lane >= hh * ATT_HEAD_DIM) & (lane < (hh + 1) * ATT_HEAD_DIM)
                qm = jnp.where(inh, q, jnp.zeros_like(q))
                s = lax.dot_general(qm, ks, NT_DIMS, preferred_element_type=F32)
                s = s + bias_ref[2 * pr + hh, variant, sb]
                m = jnp.max(s, axis=-1, keepdims=True)
                p = jnp.exp(s - m)
                l = jnp.sum(p, axis=-1, keepdims=True)
                o = jnp.dot(p.astype(BF16), vs, preferred_element_type=F32) / l
                acc = jnp.where(inh, o, acc)
            o_ref[sb * ATT_SUB_TOK:(sb + 1) * ATT_SUB_TOK, cols] = acc.astype(BF16)


def _attn_bias(rpb):
    c = np.arange(GRID_W)
    col_start = np.clip(c - WIN_W // 2, 0, GRID_W - WIN_W)
    col_in = (c[None, :] >= col_start[:, None]) & (c[None, :] < col_start[:, None] + WIN_W)
    n_dr, n_dc = 2 * WIN_H - 1, 2 * WIN_W - 1
    left = GRID_W - 1 - (WIN_W - 1)
    ext = jnp.pad(rpb.astype(F32), ((0, 0), (0, 0), (left, 2 * GRID_W - n_dc - left)))
    flat = jnp.broadcast_to(ext[:, :, None, :], (ATT_HEADS, n_dr, GRID_W, 2 * GRID_W)).reshape(ATT_HEADS, n_dr, -1)
    skew = flat[:, :, GRID_W - 1:GRID_W - 1 + GRID_W * (2 * GRID_W - 1)].reshape(ATT_HEADS, n_dr, GRID_W, 2 * GRID_W - 1)
    t = jnp.where(col_in[None, None], skew[:, :, :, :GRID_W], NEG_BIG)
    t = jnp.transpose(t, (0, 2, 1, 3)).reshape(ATT_HEADS, GRID_W, (2 * WIN_H - 1) * GRID_W)
    sub_rows = ATT_SUB_ROWS
    n_sub = ATT_RB // ATT_SUB
    dr0 = [ATT_SUB_START[sb] + (WIN_H - 1 - ATT_HALO) - (sb * ATT_SUB + r) for sb in range(n_sub) for r in range(ATT_SUB)]
    base = jnp.stack([t[:, :, d * GRID_W:(d + sub_rows) * GRID_W] for d in dr0], axis=1)
    base = base.reshape(ATT_HEADS, 1, n_sub, ATT_SUB_TOK, ATT_SUB_KEYS)
    qr = np.arange(ATT_RB)[:, None]
    u = np.asarray(ATT_SUB_START)[qr // ATT_SUB] + np.arange(sub_rows)[None, :]
    first = np.maximum(qr - ATT_HALO, 0) + ATT_HALO
    last = np.minimum(qr - ATT_HALO, 0) + ATT_HALO
    starts = np.stack([np.broadcast_to(s, u.shape) for s in (first, qr, last)])
    valid = (u[None] >= starts) & (u[None] < starts + WIN_H)
    valid = np.repeat(np.repeat(valid, GRID_W, axis=1), GRID_W, axis=2)
    valid = valid.reshape(1, ATT_VARIANTS, n_sub, ATT_SUB_TOK, ATT_SUB_KEYS)
    return jnp.where(valid, base, NEG_BIG)


def _attention(q, k, v, bias):
    b, l, _ = q.shape
    rows = l // GRID_W
    assert rows % ATT_RB == 0 and rows // ATT_RB >= 2
    nrb = rows // ATT_RB
    cur = lambda hp, bi, rb: (bi, rb, hp)
    prv = lambda hp, bi, rb: (bi, jnp.maximum(rb - 1, 0), hp)
    nxt = lambda hp, bi, rb: (bi, jnp.minimum(rb + 1, nrb - 1), hp)
    blk = (None, ATT_TOK, ATT_PAIRS * LANES)
    return pl.pallas_call(
        _attn_kernel,
        grid=(ATT_HEADS // (2 * ATT_PAIRS), b, nrb),
        in_specs=[
            pl.BlockSpec(blk, cur),
            pl.BlockSpec(blk, prv), pl.BlockSpec(blk, cur), pl.BlockSpec(blk, nxt),
            pl.BlockSpec(blk, prv), pl.BlockSpec(blk, cur), pl.BlockSpec(blk, nxt),
            pl.BlockSpec((2 * ATT_PAIRS, ATT_VARIANTS, ATT_RB // ATT_SUB, ATT_SUB_TOK, ATT_SUB_KEYS),
                         lambda hp, bi, rb: (hp, 0, 0, 0, 0),
                         pipeline_mode=pl.Buffered(1)),
        ],
        out_specs=pl.BlockSpec(blk, cur),
        out_shape=jax.ShapeDtypeStruct((b, l, ATT_WIDTH), BF16),
        compiler_params=_cparams(("parallel", "parallel", "parallel")),
        name="nbr_attention",
    )(q, k, k, k, v, v, v, bias)


CONV_TL = 512
HALO = 8


def _conv_kernel(xp_ref, xc_ref, xn_ref, w_ref, b_ref, o_ref, buf):
    i = pl.program_id(1)
    nblk = pl.num_programs(1)
    prev = xp_ref[...]
    nxt = xn_ref[...]
    buf[0:HALO] = jnp.where(i == 0, jnp.zeros_like(prev), prev)
    buf[HALO:HALO + CONV_TL] = xc_ref[...]
    buf[HALO + CONV_TL:2 * HALO + CONV_TL] = jnp.where(i == nblk - 1, jnp.zeros_like(nxt), nxt)
    x = buf[...]
    rows = CONV_TL + 2 * HALO
    w = [w_ref[j:j + 1, :] for j in range(SSD_CONV)]
    up = lambda a: pltpu.roll(a, rows - 1, 0)
    down = lambda a: pltpu.roll(a, 1, 0)
    y = x * w[2] + up(up(x * w[4]) + x * w[3]) + down(down(x * w[0]) + x * w[1])
    y = y[HALO:HALO + CONV_TL, :] + b_ref[...]
    o_ref[...] = y * (1.0 / (1.0 + jnp.exp(-y)))


def _conv_silu(xbc, conv_w, conv_b):
    b, l, ch = xbc.shape
    tl = CONV_TL
    nblk = l // tl
    per = tl // HALO
    nh = l // HALO
    return pl.pallas_call(
        _conv_kernel,
        grid=(b, nblk),
        in_specs=[
            pl.BlockSpec((None, HALO, ch), lambda bi, i: (bi, jnp.maximum(i * per - 1, 0), 0)),
            pl.BlockSpec((None, tl, ch), lambda bi, i: (bi, i, 0)),
            pl.BlockSpec((None, HALO, ch), lambda bi, i: (bi, jnp.minimum((i + 1) * per, nh - 1), 0)),
            pl.BlockSpec((SSD_CONV, ch), lambda bi, i: (0, 0)),
            pl.BlockSpec((1, ch), lambda bi, i: (0, 0)),
        ],
        out_specs=pl.BlockSpec((None, tl, ch), lambda bi, i: (bi, i, 0)),
        out_shape=jax.ShapeDtypeStruct((b, l, ch), F32),
        scratch_shapes=[pltpu.VMEM((tl + 2 * HALO, ch), F32)],
        compiler_params=_cparams(("parallel", "parallel")),
        name="conv_silu",
    )(xbc, xbc, xbc, conv_w, conv_b)


Q = SSD_CHUNK
SSD_CPS = 4


def _ssd_direction(xs_ref, bm_ref, cm_ref, dt_ref, bias, a_vec, tri, expand_m, y_ref, s_ref, fwd, rows):
    xx = dt_ref[rows, :] + bias
    dtv = jnp.maximum(xx, 0.0) + jnp.log1p(jnp.exp(-jnp.abs(xx)))
    adt = dtv * a_vec
    hi, mid, lo = _split3(adt)
    cs3 = jnp.dot(tri, jnp.concatenate([hi, mid, lo], axis=1), preferred_element_type=F32)
    cs = cs3[:, 0:LANES] + cs3[:, LANES:2 * LANES] + cs3[:, 2 * LANES:3 * LANES]

    def expand(v):
        h3, m3, _ = _split3(v)
        r = jnp.dot(jnp.concatenate([h3, m3], axis=0), expand_m, preferred_element_type=F32)
        return r[0:Q] + r[Q:2 * Q]

    dt_x = expand(dtv)
    cs_x = expand(cs)
    last = Q - 1 if fwd else 0
    tot_x = cs_x[last:last + 1, :]
    xdt = xs_ref[rows, :] * dt_x
    xdt_b = xdt.astype(BF16)
    xw = (xdt * jnp.exp(tot_x - cs_x)).astype(BF16)
    scale_off = jnp.exp(cs_x)
    chunk_decay = jnp.exp(tot_x)
    cs_t = cs.T
    row = lax.broadcasted_iota(jnp.int32, (Q, Q), 0)
    col = lax.broadcasted_iota(jnp.int32, (Q, Q), 1)
    tri_mask = (col <= row) if fwd else (col >= row)
    gw = SSD_WIDTH // SSD_GROUPS
    for g in range(SSD_GROUPS):
        bg = bm_ref[rows, g * SSD_STATE:(g + 1) * SSD_STATE]
        cg = cm_ref[rows, g * SSD_STATE:(g + 1) * SSD_STATE].astype(BF16)
        cb = lax.dot_general(cg, bg.astype(BF16), NT_DIMS, preferred_element_type=F32)
        state = s_ref[g]
        yoff = jnp.dot(cg, state.astype(BF16), preferred_element_type=F32) * scale_off[:, g * gw:(g + 1) * gw]
        bg_t = bg.T.astype(BF16)
        s_ref[g] = state * chunk_decay[:, g * gw:(g + 1) * gw] + jnp.dot(
            bg_t, xw[:, g * gw:(g + 1) * gw], preferred_element_type=F32)
        for pp in range(2):
            pair = 2 * g + pp
            xpair = xdt_b[:, pair * LANES:(pair + 1) * LANES]
            ypair = yoff[:, pp * LANES:(pp + 1) * LANES]
            for hh in range(2):
                hc = 2 * pair + hh + (0 if fwd else SSD_HEADS)
                diff = cs[:, hc:hc + 1] - cs_t[hc:hc + 1, :]
                lm = jnp.where(tri_mask, jnp.exp(jnp.minimum(diff, 0.0)), 0.0)
                mm = (cb * lm).astype(BF16)
                inh = (col >= hh * SSD_HEAD_DIM) & (col < (hh + 1) * SSD_HEAD_DIM)
                ypair = ypair + jnp.dot(mm, jnp.where(inh, xpair, jnp.zeros_like(xpair)),
                                        preferred_element_type=F32)
            y_ref[rows, pair * LANES:(pair + 1) * LANES] = ypair.astype(y_ref.dtype)


def _ssd_kernel(xs_f, bm_f, cm_f, dt_f, xs_b, bm_b, cm_b, dt_b, bias_ref, alog_ref, tri_ref, e_ref,
                yf_ref, yb_ref, sf_ref, sb_ref):
    @pl.when(pl.program_id(1) == 0)
    def _():
        sf_ref[...] = jnp.zeros_like(sf_ref)
        sb_ref[...] = jnp.zeros_like(sb_ref)

    lane = lax.broadcasted_iota(jnp.int32, (1, LANES), 1)
    bias = bias_ref[...]
    a_vec = jnp.where(lane < 2 * SSD_HEADS, -jnp.exp(alog_ref[...]), 0.0)
    for k in range(SSD_CPS):
        up = slice(k * Q, (k + 1) * Q)
        down = slice((SSD_CPS - 1 - k) * Q, (SSD_CPS - k) * Q)
        _ssd_direction(xs_f, bm_f, cm_f, dt_f, bias, a_vec, tri_ref[0], e_ref[0], yf_ref, sf_ref, True, up)
        _ssd_direction(xs_b, bm_b, cm_b, dt_b, bias, a_vec, tri_ref[1], e_ref[1], yb_ref, sb_ref, False, down)


def _ssd_scan(xbc_act, dt, bias_vec, alog_vec, tri, expand_m):
    b, l, _ = xbc_act.shape
    nc = l // (Q * SSD_CPS)
    blk = Q * SSD_CPS
    fw = lambda cb: (lambda bi, c: (bi, c, cb))
    bw = lambda cb: (lambda bi, c: (bi, nc - 1 - c, cb))
    const2 = lambda bi, c: (0, 0)
    const3 = lambda bi, c: (0, 0, 0)

    def chunk_specs(mk):
        return [
            pl.BlockSpec((None, blk, SSD_WIDTH), mk(0)),
            pl.BlockSpec((None, blk, 2 * SSD_STATE), mk(2)),
            pl.BlockSpec((None, blk, 2 * SSD_STATE), mk(3)),
            pl.BlockSpec((None, blk, LANES), mk(0)),
        ]

    return pl.pallas_call(
        _ssd_kernel,
        grid=(b, nc),
        in_specs=chunk_specs(fw) + chunk_specs(bw) + [
            pl.BlockSpec((1, LANES), const2),
            pl.BlockSpec((1, LANES), const2),
            pl.BlockSpec((2, Q, Q), const3),
            pl.BlockSpec((2, LANES, SSD_WIDTH), const3),
        ],
        out_specs=[
            pl.BlockSpec((None, blk, SSD_WIDTH), fw(0)),
            pl.BlockSpec((None, blk, SSD_WIDTH), bw(0)),
        ],
        out_shape=[jax.ShapeDtypeStruct((b, l, SSD_WIDTH), BF16)] * 2,
        scratch_shapes=[pltpu.VMEM((SSD_GROUPS, SSD_STATE, 256), F32),
                        pltpu.VMEM((SSD_GROUPS, SSD_STATE, 256), F32)],
        compiler_params=_cparams(("parallel", "arbitrary")),
        name="ssd_scan",
    )(xbc_act, xbc_act, xbc_act, dt, xbc_act, xbc_act, xbc_act, dt, bias_vec, alog_vec, tri, expand_m)


OUT_TM = 1024


def _outproj_kernel(x_ref, att_ref, xs_ref, yf_ref, yb_ref, z_ref, dx_ref, snw_ref, wo_ref, x1_ref):
    z = z_ref[...].astype(F32)
    y = xs_ref[...] * dx_ref[...] + yf_ref[...].astype(F32) + yb_ref[...].astype(F32)
    y = y * (z * (1.0 / (1.0 + jnp.exp(-z))))
    ms = jnp.mean(y * y, axis=-1, keepdims=True)
    ssd = (y * lax.rsqrt(ms + EPS) * snw_ref[...]).astype(BF16)
    mixed = jnp.dot(att_ref[...], wo_ref[0:ATT_WIDTH, :], preferred_element_type=F32)
    mixed = mixed + jnp.dot(ssd, wo_ref[ATT_WIDTH:, :], preferred_element_type=F32)
    x1_ref[...] = x_ref[...] + mixed


def _outproj(x, att, xbc_act, yf, yb, z, d_x, snw, w_out):
    n = x.shape[0]
    tm = OUT_TM
    row = lambda i: (i, 0)
    full = lambda i: (0, 0)
    return pl.pallas_call(
        _outproj_kernel,
        grid=(n // tm,),
        in_specs=[
            pl.BlockSpec((tm, D_MODEL), row),
            pl.BlockSpec((tm, ATT_WIDTH), row),
            pl.BlockSpec((tm, SSD_WIDTH), row),
            pl.BlockSpec((tm, SSD_WIDTH), row),
            pl.BlockSpec((tm, SSD_WIDTH), row),
            pl.BlockSpec((tm, SSD_WIDTH), row),
            pl.BlockSpec((1, SSD_WIDTH), full),
            pl.BlockSpec((1, SSD_WIDTH), full),
            pl.BlockSpec((D_MODEL, D_MODEL), full),
        ],
        out_specs=pl.BlockSpec((tm, D_MODEL), row),
        out_shape=jax.ShapeDtypeStruct((n, D_MODEL), F32),
        compiler_params=_cparams(("parallel",)),
        name="outproj",
    )(x, att, xbc_act, yf, yb, z, d_x, snw, w_out)


PEER_TB = 1024
PEER_PIECE = 1024
PEER_EC = 1024
PEER_M = PEER_EC // PEER_N_KEYS
XU_PARTS = 4
BF16_ROWS = 16
N_JT = PEER_N_KEYS // BF16_ROWS


SUBLANES = 8
N_ROWV = PEER_N_KEYS // SUBLANES


def _oddeven_merge_sort_pairs(n):
    out, p = [], 1
    while p < n:
        k = p
        while k >= 1:
            for j in range(k % p, n - k, 2 * k):
                for i in range(min(k, n - j - k)):
                    if (i + j) // (2 * p) == (i + j + k) // (2 * p):
                        out.append((i + j, i + j + k))
            k //= 2
        p *= 2
    return out


SORT16_PAIRS = _oddeven_merge_sort_pairs(PEER_TOPK)


def _compare_exchange(rows, i, j):
    a, b = rows[i], rows[j]
    rows[i] = jnp.maximum(a, b)
    rows[j] = jnp.minimum(a, b)


def _sort_desc(rows):
    for i, j in SORT16_PAIRS:
        if j < len(rows):
            _compare_exchange(rows, i, j)


def _top16_over_sublanes(rows):
    rows = list(rows) + [None] * (PEER_TOPK - len(rows))
    for shift in (4, 2, 1):
        other = [None if r is None else pltpu.roll(r, shift, 0) for r in rows]
        merged = []
        for k in range(PEER_TOPK):
            a, b = rows[k], other[PEER_TOPK - 1 - k]
            merged.append(b if a is None else a if b is None else jnp.maximum(a, b))
        rows = merged
        for stride in (8, 4, 2, 1):
            for k in range(PEER_TOPK):
                if k & stride == 0:
                    _compare_exchange(rows, k, k + stride)
    return rows


def _sum_over_sublanes(x):
    for shift in (4, 2, 1):
        x = x + pltpu.roll(x, shift, 0)
    return x


def _peer_select_tile(s0_s, s1_s, lanes, lt, h, r1_s, e1_s, n_s, c_s):
    rowsl = lambda ref, k: ref[k * SUBLANES:(k + 1) * SUBLANES, lanes]
    s0 = [rowsl(s0_s, k) for k in range(N_ROWV)]
    s1 = [rowsl(s1_s, k) for k in range(N_ROWV)]
    v0, v1 = list(s0), list(s1)
    _sort_desc(v0)
    _sort_desc(v1)
    v0 = _top16_over_sublanes(v0)
    v1 = _top16_over_sublanes(v1)
    sub = lax.broadcasted_iota(jnp.int32, (SUBLANES, LANES), 0)

    def pack(vals):
        out = vals[0]
        for s in range(1, SUBLANES):
            out = jnp.where(sub == s, vals[s], out)
        return out

    v1_lo, v1_hi, v0_hi = pack(v1[0:8]), pack(v1[8:16]), pack(v0[8:16])
    cand = [v0[0] + v1_lo, v0[0] + v1_hi] + [v0[a] + v1_lo for a in range(1, 8)] + [v0_hi + v1[0]]
    csort = list(cand)
    _sort_desc(csort)
    csort = _top16_over_sublanes(csort)
    tau, cmax = csort[PEER_TOPK - 1], csort[0]
    zsum = jnp.zeros_like(tau)
    for cv in cand:
        zsum = zsum + jnp.where(cv >= tau, jnp.exp(cv - cmax), 0.0)
    rz = 1.0 / _sum_over_sublanes(zsum)
    hit = lambda x: jnp.where(x >= tau, 1.0, 0.0)
    n_a = []
    for a in range(PEER_TOPK):
        reach = PEER_TOPK // (a + 1)
        if reach > SUBLANES:
            n_a.append(_sum_over_sublanes(hit(v0[a] + v1_lo) + hit(v0[a] + v1_hi)))
        elif reach > 3:
            n_a.append(_sum_over_sublanes(hit(v0[a] + v1_lo)))
        else:
            n_a.append(sum(hit(v0[a] + v1[b]) for b in range(reach)))
    for m in range(N_JT):
        packed_r, packed_e = [], []
        for k in (2 * m, 2 * m + 1):
            n = jnp.zeros_like(tau)
            r = jnp.full_like(tau, float(PEER_TOPK))
            for a in range(PEER_TOPK):
                n = jnp.where(s0[k] == v0[a], n_a[a], n)
                r = jnp.where(s1[k] == v1[a], float(a), r)
            n_s[h, k, lt] = n
            c_s[h, k, lt] = jnp.exp(s0[k] - v0[0]) * rz
            packed_r.append(r)
            packed_e.append(jnp.exp(s1[k] - v1[0]))
        r1_s[h, m, :, lanes] = jnp.concatenate(packed_r, axis=0).astype(BF16)
        e1_s[h, m, :, lanes] = jnp.concatenate(packed_e, axis=0).astype(BF16)


def _peer_kernel(x1_ref, fw_ref, wqt_ref, sk_ref, u_ref, vt_ref, nw_ref, o_ref,
                 hf_s, qt_s, r1_s, e1_s, n_s, c_s, s0_s, s1_s, acc_s):
    c = pl.program_id(1)

    @pl.when(c == 0)
    def _prepare():
        x1 = x1_ref[...]
        ms = jnp.mean(x1 * x1, axis=-1, keepdims=True)
        hf_s[...] = (x1 * lax.rsqrt(ms + EPS) * fw_ref[...]).T.astype(BF16)
        qt_s[...] = jnp.dot(wqt_ref[...], hf_s[...], preferred_element_type=F32).astype(BF16)
        acc_s[...] = jnp.zeros_like(acc_s)

        def per_head(h, carry):
            r0 = pl.multiple_of(h * 2 * PEER_D_KEY, 2 * PEER_D_KEY)
            s0 = jnp.dot(sk_ref[2 * h], qt_s[pl.ds(r0, PEER_D_KEY), :], preferred_element_type=F32)
            s1 = jnp.dot(sk_ref[2 * h + 1], qt_s[pl.ds(r0 + PEER_D_KEY, PEER_D_KEY), :],
                         preferred_element_type=F32)
            s0_s[...] = s0
            s1_s[...] = s1

            def per_tile(lt, carry2):
                lanes = pl.ds(pl.multiple_of(lt * LANES, LANES), LANES)
                _peer_select_tile(s0_s, s1_s, lanes, lt, h, r1_s, e1_s, n_s, c_s)
                return carry2

            lax.fori_loop(0, PEER_TB // LANES, per_tile, 0)
            return carry

        lax.fori_loop(0, PEER_HEADS, per_head, 0)

    n_pieces = PEER_TB // PEER_PIECE

    def expert_scores(p, part):
        rows = slice(part * (PEER_EC // XU_PARTS), (part + 1) * (PEER_EC // XU_PARTS))
        return jnp.dot(u_ref[rows, :], hf_s[:, p * PEER_PIECE:(p + 1) * PEER_PIECE],
                       preferred_element_type=F32)

    for p in range(n_pieces):
        lanes = slice(p * PEER_PIECE, (p + 1) * PEER_PIECE)
        a_parts = []
        for part in range(XU_PARTS):
            xb = expert_scores(p, part).astype(BF16)
            act = (0.5 * xb) * (1.0 + lax.erf(xb * (2.0 ** -0.5)))
            gates = []
            for ii in range(part * PEER_M // XU_PARTS, (part + 1) * PEER_M // XU_PARTS):
                g = jnp.zeros((N_JT, BF16_ROWS, PEER_PIECE), BF16)
                for h in range(PEER_HEADS):
                    tiles = range(lanes.start // LANES, lanes.stop // LANES)
                    nrow = jnp.concatenate([n_s[h, c, t, pl.ds(ii, SUBLANES, stride=0), :] for t in tiles], axis=1)
                    crow = jnp.concatenate([c_s[h, c, t, pl.ds(ii, SUBLANES, stride=0), :] for t in tiles], axis=1)
                    nb = jnp.concatenate([nrow, nrow], axis=0).astype(BF16)
                    cb = jnp.concatenate([crow, crow], axis=0).astype(BF16)
                    sel = jnp.where(r1_s[h, :, :, lanes] < nb[None], e1_s[h, :, :, lanes], jnp.zeros((), BF16))
                    g = g + sel * cb[None]
                gates.append(g.reshape(PEER_N_KEYS, PEER_PIECE))
            a_parts.append(act * jnp.concatenate(gates, axis=0))
        a_t = jnp.concatenate(a_parts, axis=0)
        acc_s[:, lanes] += jnp.dot(vt_ref[...], a_t, preferred_element_type=F32)

    @pl.when(c == pl.num_programs(1) - 1)
    def _finish():
        y = x1_ref[...] + acc_s[...].T
        ms = jnp.mean(y * y, axis=-1, keepdims=True)
        o_ref[...] = y * lax.rsqrt(ms + EPS) * nw_ref[...]


def _peer(x1, fw, wq_t, sub_keys, exp_u, exp_vt, nw):
    n = x1.shape[0]
    tb, ec = PEER_TB, PEER_EC
    n_chunks = PEER_N_EXPERTS // ec
    tok = lambda t, c: (t, 0)
    const2 = lambda t, c: (0, 0)
    head_tiles = (PEER_HEADS, N_JT, BF16_ROWS, tb)
    key_tiles = (PEER_HEADS, N_ROWV, tb // LANES, SUBLANES, LANES)
    assert PEER_M == SUBLANES
    return pl.pallas_call(
        _peer_kernel,
        grid=(n // tb, n_chunks),
        in_specs=[
            pl.BlockSpec((tb, D_MODEL), tok, pipeline_mode=pl.Buffered(1)),
            pl.BlockSpec((1, D_MODEL), const2),
            pl.BlockSpec((2 * PEER_HEADS * PEER_D_KEY, D_MODEL), const2, pipeline_mode=pl.Buffered(1)),
            pl.BlockSpec((2 * PEER_HEADS, PEER_N_KEYS, PEER_D_KEY), lambda t, c: (0, 0, 0)),
            pl.BlockSpec((ec, D_MODEL), lambda t, c: (c, 0)),
            pl.BlockSpec((D_MODEL, ec), lambda t, c: (0, c)),
            pl.BlockSpec((1, D_MODEL), const2),
        ],
        out_specs=pl.BlockSpec((tb, D_MODEL), tok),
        out_shape=jax.ShapeDtypeStruct((n, D_MODEL), F32),
        scratch_shapes=[
            pltpu.VMEM((D_MODEL, tb), BF16),
            pltpu.VMEM((2 * PEER_HEADS * PEER_D_KEY, tb), BF16),
            pltpu.VMEM(head_tiles, BF16),
            pltpu.VMEM(head_tiles, BF16),
            pltpu.VMEM(key_tiles, F32),
            pltpu.VMEM(key_tiles, F32),
            pltpu.VMEM((PEER_N_KEYS, tb), F32),
            pltpu.VMEM((PEER_N_KEYS, tb), F32),
            pltpu.VMEM((D_MODEL, tb), F32),
        ],
        compiler_params=_cparams(("parallel", "arbitrary")),
        name="peer",
    )(x1, fw, wq_t, sub_keys, exp_u, exp_vt, nw)


def _lane_vec(two_by_heads):
    v = two_by_heads.astype(F32).reshape(1, 2 * SSD_HEADS)
    return jnp.pad(v, ((0, 0), (0, LANES - 2 * SSD_HEADS)))


def _trunk(x, p):
    b, l, d = x.shape
    n = b * l
    xf = x.reshape(n, d)
    q, k, v, z, xbc, dt = _inproj(xf, p["norm_mix_w"], p["w_main"], p["w_dt"])
    att = _attention(q.reshape(b, l, -1), k.reshape(b, l, -1), v.reshape(b, l, -1), p["att_bias"])
    xbc_act = _conv_silu(xbc.reshape(b, l, -1), p["conv_w"], p["conv_b"])
    yf, yb = _ssd_scan(xbc_act, dt.reshape(b, l, -1), p["dt_bias_vec"], p["a_log_vec"], p["tri"], p["expand"])
    x1 = _outproj(xf, att.reshape(n, -1), xbc_act.reshape(n, -1), yf.reshape(n, -1), yb.reshape(n, -1),
                  z, p["d_x"], p["ssd_norm_w"], p["w_out"])
    y = _peer(x1, p["norm_ffn_w"], p["wq_t"], p["sub_keys"], p["exp_u"], p["exp_vt"], p["norm_final_w"])
    return y.reshape(b, l, d)


def kernel(x_prompt, x_sample, norm_mix_w, w_in, rpb, conv_w, conv_b, dt_bias, a_log, d_skip, ssd_norm_w,
           w_out, norm_ffn_w, w_query, sub_keys, expert_u, expert_v, norm_final_w):
    assert w_in.shape[0] == 1, "single layer"
    w_in0 = w_in[0]
    w_dt = w_in0[:, MAIN_COLS:]
    w_dt = jnp.concatenate([w_dt, w_dt, jnp.zeros((D_MODEL, LANES - 2 * SSD_HEADS), w_dt.dtype)], axis=1)
    r = jnp.arange(Q)
    tril = (r[None, :] <= r[:, None])
    tri = jnp.stack([tril, tril.T]).astype(BF16)
    hid = jnp.arange(SSD_WIDTH) // SSD_HEAD_DIM
    lane = jnp.arange(LANES)
    expand = jnp.stack([lane[:, None] == hid[None, :],
                        lane[:, None] == hid[None, :] + SSD_HEADS]).astype(BF16)
    p = {
        "norm_mix_w": norm_mix_w[0].reshape(1, -1),
        "w_main": w_in0[:, :MAIN_COLS].astype(BF16),
        "w_dt": w_dt.astype(BF16),
        "att_bias": _attn_bias(rpb[0]),
        "conv_w": conv_w[0],
        "conv_b": conv_b[0].reshape(1, -1),
        "dt_bias_vec": _lane_vec(dt_bias[0]),
        "a_log_vec": _lane_vec(a_log[0]),
        "tri": tri,
        "expand": expand,
        "d_x": jnp.repeat(d_skip[0].astype(F32), SSD_HEAD_DIM).reshape(1, -1),
        "ssd_norm_w": ssd_norm_w[0].reshape(1, -1),
        "w_out": w_out[0].astype(BF16),
        "norm_ffn_w": norm_ffn_w[0].reshape(1, -1),
        "wq_t": w_query[0].T.astype(BF16),
        "sub_keys": sub_keys[0].reshape(2 * PEER_HEADS, PEER_N_KEYS, PEER_D_KEY).astype(BF16),
        "exp_u": expert_u[0].astype(BF16),
        "exp_vt": expert_v[0].T.astype(BF16),
        "norm_final_w": norm_final_w.reshape(1, -1),
    }
    return (_trunk(x_prompt, p), _trunk(x_sample, p))
```
